```python
import jax, jax.numpy as jnp
from jax import lax
import numpy as np

D_MODEL = 1024
BATCH = 4
SEQ = 4096
DEPTH = 1

CHUNK = 64
MIX_WIDTH = D_MODEL
HG_HEADS = 4
HG_WIDTH = MIX_WIDTH // 2
HG_HEAD_DIM = HG_WIDTH // HG_HEADS
POOL_WINDOWS = (2, 4, 8, 16)
POOL_GROUPS = 4
POOL_WIDTH = MIX_WIDTH - HG_WIDTH
POOL_GDIM = POOL_WIDTH // POOL_GROUPS
IN_WIDTH = 4 * HG_WIDTH + POOL_WIDTH
N_BRANCH = 2
MOE_GROUPS = 4
MOE_EXPERTS_PER_GROUP = 8
N_EXPERTS = MOE_GROUPS * MOE_EXPERTS_PER_GROUP
MOE_TOP_K = 2
EXPERT_FF = D_MODEL // 2
MOE_BLOCK = 256
EPS = 1e-6
ADA_INIT = 0.5

kernel_name = "hybrid_hgrn2_pool_hmoe_block"


def rms_norm(x, g):
    xf = x.astype(jnp.float32)
    y = xf * lax.rsqrt(jnp.mean(xf * xf, axis=-1, keepdims=True) + EPS)
    return (y * g.astype(jnp.float32)).astype(x.dtype)


def hgrn2_mixer(q_raw, f_raw, v_raw, g_raw, lb, norm_g):
    f32 = jnp.float32
    Bsz, S, _ = q_raw.shape
    nC = S // CHUNK
    heads = lambda t: t.astype(f32).reshape(Bsz, S, HG_HEADS, HG_HEAD_DIM)
    q = jax.nn.silu(heads(q_raw))
    lbh = lb.astype(f32).reshape(HG_HEADS, HG_HEAD_DIM)
    f = lbh + (1.0 - lbh) * jax.nn.sigmoid(heads(f_raw))
    k = 1.0 - f
    log_f = jnp.log(f)

    def chunks(t):
        return t.reshape(Bsz, nC, CHUNK, HG_HEADS, HG_HEAD_DIM).transpose(1, 0, 3, 2, 4)

    qc, kc, vc = chunks(q), chunks(k), chunks(heads(v_raw))
    bc = jnp.cumsum(chunks(log_f), axis=3)
    causal = jnp.tril(jnp.ones((CHUNK, CHUNK), bool))[:, :, None]

    def step(state, inp):
        qb, kb, vb, bb = inp
        diff = bb[:, :, :, None, :] - bb[:, :, None, :, :]
        decay = jnp.exp(jnp.where(causal, diff, -jnp.inf))
        scores = jnp.sum(qb[:, :, :, None, :] * kb[:, :, None, :, :] * decay, axis=-1)
        o = (jnp.einsum('bhts,bhsv->bhtv', scores, vb)
             + jnp.einsum('bhtk,bhkv->bhtv', qb * jnp.exp(bb), state))
        b_last = bb[:, :, -1:, :]
        state = (jnp.exp(b_last[:, :, 0, :, None]) * state
                 + jnp.einsum('bhsk,bhsv->bhkv', kb * jnp.exp(b_last - bb), vb))
        return state, o

    s0 = jnp.zeros((Bsz, HG_HEADS, HG_HEAD_DIM, HG_HEAD_DIM), f32)
    _, o = lax.scan(step, s0, (qc, kc, vc, bc))
    o = o.transpose(1, 0, 3, 2, 4).reshape(Bsz, S, HG_HEADS, HG_HEAD_DIM)
    o = o * lax.rsqrt(jnp.mean(o * o, axis=-1, keepdims=True) + EPS)
    o = o.reshape(Bsz, S, HG_WIDTH) * norm_g.astype(f32) * jax.nn.silu(g_raw.astype(f32))
    return o.astype(q_raw.dtype)


def multiscale_pool(u, w_grp, ls):
    f32 = jnp.float32
    Bsz, S, _ = u.shape
    uf = u.astype(f32).reshape(Bsz, S, POOL_GROUPS, POOL_GDIM)
    cs = jnp.pad(jnp.cumsum(uf, axis=1), ((0, 0), (1, 0), (0, 0), (0, 0)))
    win = jnp.array(POOL_WINDOWS, jnp.int32)
    t = jnp.arange(S, dtype=jnp.int32)[:, None]
    lo = jnp.maximum(t + 1 - win[None, :], 0)
    grp = jnp.arange(POOL_GROUPS, dtype=jnp.int32)[None, :]
    window_sum = cs[:, 1:] - cs[:, lo, grp]
    count = jnp.minimum(t + 1, win[None, :]).astype(f32)
    pooled = window_sum / count[None, :, :, None] - uf
    mixed = jnp.einsum('bsgc,gcd->bsgd', pooled, w_grp.astype(f32)).reshape(Bsz, S, POOL_WIDTH)
    return (mixed * ls.astype(f32)).astype(u.dtype)


def hierarchical_moe(h, w_rg, b_rg, w_re, b_re, w_gate_e, w_up_e, w_down_e):
    f32 = jnp.float32
    Bsz, S, D = h.shape
    N = Bsz * S
    hf = h.reshape(N, D)
    p_g = jax.nn.softmax((hf @ w_rg + b_rg).astype(f32), axis=-1)
    p_top_g, g_idx = lax.top_k(p_g, 1)
    e_logits = (hf @ w_re + b_re).astype(f32).reshape(N, MOE_GROUPS, MOE_EXPERTS_PER_GROUP)
    e_sel = jnp.take_along_axis(e_logits, g_idx[:, :, None], axis=1)[:, 0]
    p_top_e, e_local = lax.top_k(jax.nn.softmax(e_sel, axis=-1), MOE_TOP_K)
    weights = p_top_g * p_top_e / jnp.sum(p_top_e, axis=-1, keepdims=True)
    e_global = g_idx * MOE_EXPERTS_PER_GROUP + e_local

    A = N * MOE_TOP_K
    e_flat = e_global.reshape(A).astype(jnp.int32)
    tok_flat = jnp.repeat(jnp.arange(N, dtype=jnp.int32), MOE_TOP_K)
    w_flat = weights.reshape(A)
    order = jnp.argsort(e_flat)
    e_sorted = e_flat[order]
    counts = jnp.bincount(e_flat, length=N_EXPERTS)
    pcounts = ((counts + MOE_BLOCK - 1) // MOE_BLOCK) * MOE_BLOCK
    start = jnp.cumsum(counts) - counts
    pend = jnp.cumsum(pcounts)
    pstart = pend - pcounts
    dest = pstart[e_sorted] + jnp.arange(A, dtype=jnp.int32) - start[e_sorted]
    n_blocks = -(-A // MOE_BLOCK) + N_EXPERTS
    P = n_blocks * MOE_BLOCK
    row_tok = jnp.full((P,), N, jnp.int32).at[dest].set(tok_flat[order])
    row_w = jnp.zeros((P,), f32).at[dest].set(w_flat[order])
    block_e = jnp.minimum(jnp.searchsorted(pend, jnp.arange(n_blocks) * MOE_BLOCK, side='right'),
                          N_EXPERTS - 1).astype(jnp.int32)
    x_pad = jnp.concatenate([hf, jnp.zeros((1, D), hf.dtype)], axis=0)
    xb = x_pad[row_tok].reshape(n_blocks, MOE_BLOCK, D)

    def expert_block(args):
        xblk, e = args
        return (jax.nn.silu(xblk @ w_gate_e[e]) * (xblk @ w_up_e[e])) @ w_down_e[e]

    yb = lax.map(expert_block, (xb, block_e))
    y = jax.ops.segment_sum(yb.reshape(P, D).astype(f32) * row_w[:, None], row_tok,
                            num_segments=N + 1)[:N]
    return y.reshape(Bsz, S, D).astype(h.dtype)


def setup_inputs(seed: int = 0) -> dict:
    key = jax.random.key(seed)
    ks = jax.random.split(key, 26)
    f32 = jnp.float32
    D = D_MODEL

    def nrm(k, shape, scale):
        return scale * jax.random.normal(k, shape, f32)

    return {
        "x": nrm(ks[0], (BATCH, SEQ, D), 1.0),
        "c": nrm(ks[1], (BATCH, D), 1.0),
        "w_ada": nrm(ks[2], (DEPTH, D, 6 * D), ADA_INIT * D ** -0.5),
        "b_ada": nrm(ks[3], (DEPTH, 6 * D), 0.02),
        "g_pre_mix": 1.0 + nrm(ks[4], (DEPTH, D), 0.02),
        "g_post_mix": 1.0 + nrm(ks[5], (DEPTH, D), 0.02),
        "w_in": nrm(ks[6], (DEPTH, D, IN_WIDTH), D ** -0.5),
        "hg_lb_logits": nrm(ks[7], (DEPTH + 1, HG_WIDTH), 0.1),
        "hg_norm_g": 1.0 + nrm(ks[8], (DEPTH, HG_WIDTH), 0.02),
        "pool_w": nrm(ks[9], (DEPTH, POOL_GROUPS, POOL_GDIM, POOL_GDIM), POOL_GDIM ** -0.5),
        "pool_scale": 1.0 + nrm(ks[10], (DEPTH, POOL_WIDTH), 0.02),
        "w_branch_hg": nrm(ks[11], (DEPTH, HG_WIDTH, D), HG_WIDTH ** -0.5),
        "w_branch_pool": nrm(ks[12], (DEPTH, POOL_WIDTH, D), POOL_WIDTH ** -0.5),
        "w_gate": nrm(ks[13], (DEPTH, D, N_BRANCH * D), D ** -0.5),
        "b_gate": nrm(ks[14], (DEPTH, N_BRANCH * D), 0.02),
        "w_out": nrm(ks[15], (DEPTH, D, D), D ** -0.5),
        "g_pre_ffn": 1.0 + nrm(ks[16], (DEPTH, D), 0.02),
        "g_post_ffn": 1.0 + nrm(ks[17], (DEPTH, D), 0.02),
        "w_router_group": nrm(ks[18], (DEPTH, D, MOE_GROUPS), D ** -0.5),
        "b_router_group": nrm(ks[19], (DEPTH, MOE_GROUPS), 0.01),
        "w_router_expert": nrm(ks[20], (DEPTH, D, N_EXPERTS), D ** -0.5),
        "b_router_expert": nrm(ks[21], (DEPTH, N_EXPERTS), 0.01),
        "w_exp_gate": nrm(ks[22], (DEPTH, N_EXPERTS, D, EXPERT_FF), D ** -0.5),
        "w_exp_up": nrm(ks[23], (DEPTH, N_EXPERTS, D, EXPERT_FF), D ** -0.5),
        "w_exp_down": nrm(ks[24], (DEPTH, N_EXPERTS, EXPERT_FF, D), EXPERT_FF ** -0.5),
    }


def reference(x, c, w_ada, b_ada, g_pre_mix, g_post_mix, w_in, hg_lb_logits, hg_norm_g, pool_w,
              pool_scale, w_branch_hg, w_branch_pool, w_gate, b_gate, w_out, g_pre_ffn, g_post_ffn,
              w_router_group, b_router_group, w_router_expert, b_router_expert, w_exp_gate,
              w_exp_up, w_exp_down):
    lb_all = jnp.cumsum(jax.nn.softmax(hg_lb_logits.astype(jnp.float32), axis=0), axis=0)
    cond = jax.nn.silu(c)
    h = x
    for l in range(DEPTH):
        ada = (cond @ w_ada[l] + b_ada[l])[:, None, :]
        sh1, sc1, gt1, sh2, sc2, gt2 = jnp.split(ada, 6, axis=-1)

        hn = rms_norm(h, g_pre_mix[l]) * (1.0 + sc1) + sh1
        proj = hn @ w_in[l]
        q, f, i, g, u = jnp.split(proj, [HG_WIDTH, 2 * HG_WIDTH, 3 * HG_WIDTH, 4 * HG_WIDTH], axis=-1)
        y_hg = hgrn2_mixer(q, f, i, g, lb_all[l], hg_norm_g[l]) @ w_branch_hg[l]
        y_pool = multiscale_pool(u, pool_w[l], pool_scale[l]) @ w_branch_pool[l]
        gate_hg, gate_pool = jnp.split(jax.nn.sigmoid(hn @ w_gate[l] + b_gate[l]), 2, axis=-1)
        mix = (gate_hg * y_hg + gate_pool * y_pool) @ w_out[l]
        h = h + gt1 * rms_norm(mix, g_post_mix[l])

        hn = rms_norm(h, g_pre_ffn[l]) * (1.0 + sc2) + sh2
        y = hierarchical_moe(hn, w_router_group[l], b_router_group[l], w_router_expert[l],
                             b_router_expert[l], w_exp_gate[l], w_exp_up[l], w_exp_down[l])
        h = h + gt2 * rms_norm(y, g_post_ffn[l])
    return h
```

```python
import functools

import jax
import jax.numpy as jnp
from jax import lax
from jax.experimental import pallas as pl
from jax.experimental.pallas import tpu as pltpu

F32 = jnp.float32
BF16 = jnp.bfloat16
HIGHEST = lax.Precision.HIGHEST

CHUNK = 64
HG_HEADS = 4
POOL_WINDOWS = (2, 4, 8, 16)
POOL_GROUPS = 4
MOE_GROUPS = 4
MOE_EPG = 8
N_EXPERTS = MOE_GROUPS * MOE_EPG
MOE_BLOCK = 256
EPS = 1e-6

MIX_ROWS = 256
POOL_HALO = 128
ROUTE_COLS = 512
MOVE_ROWS = 512
LOG_DECAY_FLOOR = -80.0
VMEM_LIMIT = 56 * 1024 * 1024

NT_DIMS = (((1,), (1,)), ((), ()))
TN_DIMS = (((0,), (0,)), ((), ()))


def _sigmoid(v):
    return 1.0 / (1.0 + jnp.exp(-v))


def _rms_norm(v, g):
    return v * lax.rsqrt(jnp.mean(v * v, axis=-1, keepdims=True) + EPS) * g


def _ada_kernel(c_ref, w_ref, b_ref, o_ref):
    c = c_ref[...]
    cond = c * _sigmoid(c)
    o_ref[...] = jnp.dot(cond, w_ref[...], preferred_element_type=F32, precision=HIGHEST) + b_ref[...]


def _ada(c, w_ada, b_ada):
    bsz, d = c.shape
    n_out = w_ada.shape[1]
    return pl.pallas_call(
        _ada_kernel,
        grid=(n_out // d,),
        in_specs=[pl.BlockSpec((bsz, d), lambda i: (0, 0)),
                  pl.BlockSpec((d, d), lambda i: (0, i)),
                  pl.BlockSpec((1, d), lambda i: (0, i))],
        out_specs=pl.BlockSpec((bsz, d), lambda i: (0, i)),
        out_shape=jax.ShapeDtypeStruct((bsz, n_out), F32),
        name="ada",
    )(c, w_ada, b_ada.reshape(1, n_out))


def _mix_kernel(x_ref, ada_ref, gpre_ref, gpost_ref, gffn_ref, w1_ref, bgate_ref, lbl_ref, hgn_ref,
                tril_ref, band_ref, poolw_ref, pools_ref, wbh_ref, wbp_ref, wout_ref, wr_ref, br_ref,
                h1_ref, hn2_ref, eid_ref, rw_ref,
                st_ref, ext_ref, o_scr):
    t_rows, d = x_ref.shape
    hgw = hgn_ref.shape[1]
    hd = hgw // HG_HEADS
    pw = pools_ref.shape[1]
    gd = pw // POOL_GROUPS
    j = pl.program_id(1)

    @pl.when(j == 0)
    def _():
        st_ref[...] = jnp.zeros_like(st_ref)
        ext_ref[t_rows:t_rows + POOL_HALO, :] = jnp.zeros((POOL_HALO, pw), BF16)

    x = x_ref[...]
    ada = ada_ref[...]
    sh1, sc1, gt1 = ada[0:1], ada[1:2], ada[2:3]
    sh2, sc2 = ada[3:4], ada[4:5]

    hn = _rms_norm(x, gpre_ref[...]) * (1.0 + sc1) + sh1
    hb = hn.astype(BF16)
    c_u = 4 * hgw
    c_gate = c_u + pw
    qfig = jnp.dot(hb, w1_ref[:, 0:c_u], preferred_element_type=F32)
    u = jnp.dot(hb, w1_ref[:, c_u:c_gate], preferred_element_type=F32)
    gl = jnp.dot(hb, w1_ref[:, c_gate:], preferred_element_type=F32) + bgate_ref[...]

    qr, fr = qfig[:, 0:hgw], qfig[:, hgw:2 * hgw]
    vr, gr = qfig[:, 2 * hgw:3 * hgw], qfig[:, 3 * hgw:4 * hgw]
    lbl = lbl_ref[...]
    lmax = jnp.maximum(lbl[0:1], lbl[1:2])
    e0 = jnp.exp(lbl[0:1] - lmax)
    lb = e0 / (e0 + jnp.exp(lbl[1:2] - lmax))
    q = qr * _sigmoid(qr)
    f = lb + (1.0 - lb) * _sigmoid(fr)
    k = 1.0 - f
    lf = jnp.log(f)
    lf_hi = lf.astype(BF16)
    r1 = lf - lf_hi.astype(F32)
    lf_mid = r1.astype(BF16)
    lf_lo = (r1 - lf_mid.astype(F32)).astype(BF16)
    tril = tril_ref[...]
    b = (jnp.dot(tril, lf_hi, preferred_element_type=F32)
         + jnp.dot(tril, lf_mid, preferred_element_type=F32)
         + jnp.dot(tril, lf_lo, preferred_element_type=F32))
    bc = jnp.maximum(b, LOG_DECAY_FLOOR)
    qt = (q * jnp.exp(bc)).astype(BF16)
    kt = (k * jnp.exp(-bc)).astype(BF16)
    vb = vr.astype(BF16)
    og = hgn_ref[...] * (gr * _sigmoid(gr))

    row = lax.broadcasted_iota(jnp.int32, (CHUNK, CHUNK), 0)
    col = lax.broadcasted_iota(jnp.int32, (CHUNK, CHUNK), 1)
    causal = row >= col
    for ci in range(t_rows // CHUNK):
        r0 = ci * CHUNK
        for h in range(HG_HEADS):
            c0 = h * hd
            qt_c = qt[r0:r0 + CHUNK, c0:c0 + hd]
            kt_c = kt[r0:r0 + CHUNK, c0:c0 + hd]
            v_c = vb[r0:r0 + CHUNK, c0:c0 + hd]
            b_c = b[r0:r0 + CHUNK, c0:c0 + hd]
            k_c = k[r0:r0 + CHUNK, c0:c0 + hd]
            b_last = b_c[CHUNK - 1:CHUNK, :]
            st_t = st_ref[h]
            s = lax.dot_general(qt_c, kt_c, NT_DIMS, preferred_element_type=F32)
            s = jnp.where(causal, s, 0.0)
            o = (jnp.dot(s.astype(BF16), v_c, preferred_element_type=F32)
                 + lax.dot_general(qt_c, st_t.astype(BF16), NT_DIMS, preferred_element_type=F32))
            kh = (k_c * jnp.exp(b_last - b_c)).astype(BF16)
            upd = lax.dot_general(v_c, kh, TN_DIMS, preferred_element_type=F32)
            st_ref[h] = jnp.exp(b_last) * st_t + upd
            o = o * lax.rsqrt(jnp.mean(o * o, axis=-1, keepdims=True) + EPS)
            o_scr[r0:r0 + CHUNK, c0:c0 + hd] = (o * og[r0:r0 + CHUNK, c0:c0 + hd]).astype(BF16)
    y_hg = jnp.dot(o_scr[...], wbh_ref[...], preferred_element_type=F32)

    ext_ref[0:POOL_HALO, :] = ext_ref[t_rows:t_rows + POOL_HALO, :]
    ext_ref[POOL_HALO:POOL_HALO + t_rows, :] = u.astype(BF16)
    pos = j * t_rows + lax.broadcasted_iota(jnp.int32, (t_rows, gd), 0)
    mixed = []
    for g in range(POOL_GROUPS):
        c0 = g * gd
        wsum = jnp.dot(band_ref[g], ext_ref[:, c0:c0 + gd], preferred_element_type=F32)
        cnt = jnp.minimum(pos + 1, POOL_WINDOWS[g]).astype(F32)
        pooled = wsum / cnt - u[:, c0:c0 + gd]
        mixed.append(jnp.dot(pooled.astype(BF16), poolw_ref[g], preferred_element_type=F32))
    mixed = jnp.concatenate(mixed, axis=1) * pools_ref[...]
    y_pool = jnp.dot(mixed.astype(BF16), wbp_ref[...], preferred_element_type=F32)

    gate = _sigmoid(gl)
    merged = gate[:, 0:d] * y_hg + gate[:, d:2 * d] * y_pool
    mix = jnp.dot(merged.astype(BF16), wout_ref[...], preferred_element_type=F32)
    h1 = x + gt1 * _rms_norm(mix, gpost_ref[...])
    h1_ref[...] = h1

    hn2 = _rms_norm(h1, gffn_ref[...]) * (1.0 + sc2) + sh2
    hn2_ref[...] = hn2
    lt = lax.dot_general(wr_ref[...], hn2, NT_DIMS, preferred_element_type=F32, precision=HIGHEST)
    lt = lt + br_ref[...]
    lg = lt[0:MOE_GROUPS]
    gmax = jnp.max(lg, axis=0, keepdims=True)
    p_g = 1.0 / jnp.sum(jnp.exp(lg - gmax), axis=0, keepdims=True)
    gi = lax.broadcasted_iota(jnp.int32, lg.shape, 0).astype(F32)
    g_idx = jnp.min(jnp.where(lg == gmax, gi, float(MOE_GROUPS)), axis=0, keepdims=True)
    le = lt[8:8 + MOE_EPG]
    for g in range(1, MOE_GROUPS):
        le = jnp.where(g_idx == float(g), lt[8 + g * MOE_EPG:8 + (g + 1) * MOE_EPG], le)
    ei = lax.broadcasted_iota(jnp.int32, le.shape, 0).astype(F32)
    m1 = jnp.max(le, axis=0, keepdims=True)
    i1 = jnp.min(jnp.where(le == m1, ei, float(MOE_EPG)), axis=0, keepdims=True)
    le2 = jnp.where(ei == i1, -jnp.inf, le)
    m2 = jnp.max(le2, axis=0, keepdims=True)
    i2 = jnp.min(jnp.where(le2 == m2, ei, float(MOE_EPG)), axis=0, keepdims=True)
    r = jnp.exp(m2 - m1)
    w_first = p_g / (1.0 + r)
    eid_ref[0:1, :] = (g_idx * MOE_EPG + i1).astype(jnp.int32)
    eid_ref[1:2, :] = (g_idx * MOE_EPG + i2).astype(jnp.int32)
    rw_ref[0:1, :] = w_first
    rw_ref[1:2, :] = w_first * r


def _band_matrices(t_rows):
    t = jnp.arange(t_rows)[:, None] + POOL_HALO
    jx = jnp.arange(t_rows + POOL_HALO)[None, :]
    return jnp.stack([((jx <= t) & (jx > t - w)) for w in POOL_WINDOWS]).astype(BF16)


def _chunk_tril(t_rows):
    r = jnp.arange(t_rows)[:, None]
    c = jnp.arange(t_rows)[None, :]
    return ((r >= c) & (r // CHUNK == c // CHUNK)).astype(BF16)


def _mixer(x, ada, g_pre, g_post, g_ffn, w1, b_gate, lb_logits, hg_norm_g, pool_w, pool_scale,
           w_bh, w_bp, w_out, w_router_t, b_router):
    bsz, seq, d = x.shape
    t_rows = MIX_ROWS
    n_t = seq // t_rows
    n_tok = bsz * seq
    hgw = hg_norm_g.shape[1]
    pw = pool_scale.shape[1]
    n_r = w_router_t.shape[0]

    def const(shape):
        return pl.BlockSpec(shape, lambda b, j: (0,) * len(shape), pipeline_mode=pl.Buffered(1))

    in_specs = [
        pl.BlockSpec((None, t_rows, d), lambda b, j: (b, j, 0)),
        pl.BlockSpec((None, 6, d), lambda b, j: (b, 0, 0)),
        const((1, d)), const((1, d)), const((1, d)),
        const(w1.shape), const((1, 2 * d)),
        const(lb_logits.shape), const((1, hgw)),
        const((t_rows, t_rows)), const((POOL_GROUPS, t_rows, t_rows + POOL_HALO)),
        const(pool_w.shape), const((1, pw)),
        const(w_bh.shape), const(w_bp.shape), const(w_out.shape),
        const(w_router_t.shape), const((n_r, 1)),
    ]
    out_specs = [
        pl.BlockSpec((None, t_rows, d), lambda b, j: (b, j, 0)),
        pl.BlockSpec((None, t_rows, d), lambda b, j: (b, j, 0)),
        pl.BlockSpec((2, t_rows), lambda b, j: (0, b * n_t + j)),
        pl.BlockSpec((2, t_rows), lambda b, j: (0, b * n_t + j)),
    ]
    out_shape = [
        jax.ShapeDtypeStruct((bsz, seq, d), F32),
        jax.ShapeDtypeStruct((bsz, seq, d), F32),
        jax.ShapeDtypeStruct((2, n_tok), jnp.int32),
        jax.ShapeDtypeStruct((2, n_tok), F32),
    ]
    scratch = [
        pltpu.VMEM((HG_HEADS, hgw // HG_HEADS, hgw // HG_HEADS), F32),
        pltpu.VMEM((t_rows + POOL_HALO, pw), BF16),
        pltpu.VMEM((t_rows, hgw), BF16),
    ]
    return pl.pallas_call(
        _mix_kernel,
        grid=(bsz, n_t),
        in_specs=in_specs, out_specs=out_specs, out_shape=out_shape, scratch_shapes=scratch,
        compiler_params=pltpu.CompilerParams(dimension_semantics=("arbitrary", "arbitrary"),
                                             vmem_limit_bytes=VMEM_LIMIT),
        name="mixer",
    )(x, ada, g_pre, g_post, g_ffn, w1, b_gate, lb_logits, hg_norm_g, _chunk_tril(t_rows),
      _band_matrices(t_rows), pool_w, pool_scale, w_bh, w_bp, w_out, w_router_t, b_router)


def _route_kernel(eid_ref, dest_ref, blk_ref, meta_ref, cnt_ref, run_ref):
    phase = pl.program_id(0)
    i = pl.program_id(1)
    cols = eid_ref.shape[1]
    eidx = lax.broadcasted_iota(jnp.int32, (N_EXPERTS, cols), 0)
    hot0 = eidx == eid_ref[0:1, :]
    hot1 = eidx == eid_ref[1:2, :]
    both = jnp.where(hot0 | hot1, 1.0, 0.0)
    tile_cnt = jnp.sum(both, axis=1, keepdims=True)

    @pl.when((phase == 0) & (i == 0))
    def _():
        cnt_ref[...] = jnp.zeros_like(cnt_ref)

    @pl.when(phase == 0)
    def _():
        cnt_ref[...] += tile_cnt

    @pl.when((phase == 1) & (i == 0))
    def _():
        nblk = jnp.floor((cnt_ref[...] + float(MOE_BLOCK - 1)) * (1.0 / MOE_BLOCK))
        nblk_f = jnp.broadcast_to(nblk, (N_EXPERTS, 128))
        er = lax.broadcasted_iota(jnp.int32, (N_EXPERTS, N_EXPERTS), 0)
        ec = lax.broadcasted_iota(jnp.int32, (N_EXPERTS, N_EXPERTS), 1)
        lower = jnp.where(ec < er, 1.0, 0.0).astype(BF16)
        start_blk = jnp.dot(lower, nblk_f.astype(BF16), preferred_element_type=F32)
        run_ref[...] = start_blk[:, 0:1] * float(MOE_BLOCK)
        end_blk = start_blk + nblk_f
        lane = lax.broadcasted_iota(jnp.int32, (N_EXPERTS, blk_ref.shape[1]), 1).astype(F32)
        done = jnp.where(end_blk[:, 0:1] <= lane, 1.0, 0.0)
        blk_ref[...] = jnp.minimum(jnp.sum(done, axis=0, keepdims=True),
                                   float(N_EXPERTS - 1)).astype(jnp.int32)
        meta_ref[...] = jnp.broadcast_to(end_blk[N_EXPERTS - 1:N_EXPERTS, 0:1],
                                         meta_ref.shape).astype(jnp.int32)

    @pl.when(phase == 1)
    def _():
        r = lax.broadcasted_iota(jnp.int32, (cols, cols), 0)
        c = lax.broadcasted_iota(jnp.int32, (cols, cols), 1)
        before = jnp.where(r < c, 1.0, 0.0).astype(BF16)
        prefix = jnp.dot(both.astype(BF16), before, preferred_element_type=F32)
        slot = run_ref[...] + prefix
        dest_ref[0:1, :] = jnp.sum(jnp.where(hot0, slot, 0.0), axis=0, keepdims=True).astype(jnp.int32)
        dest_ref[1:2, :] = jnp.sum(jnp.where(hot1, slot, 0.0), axis=0, keepdims=True).astype(jnp.int32)
        run_ref[...] += tile_cnt


def _route(eid, n_blocks):
    n_tok = eid.shape[1]
    cols = ROUTE_COLS
    blk_lanes = pl.cdiv(n_blocks, 128) * 128
    return pl.pallas_call(
        _route_kernel,
        grid=(2, n_tok // cols),
        in_specs=[pl.BlockSpec((2, cols), lambda p, i: (0, i))],
        out_specs=[pl.BlockSpec((2, cols), lambda p, i: (0, i * p)),
                   pl.BlockSpec((1, blk_lanes), lambda p, i: (0, 0)),
                   pl.BlockSpec((1, 128), lambda p, i: (0, 0))],
        out_shape=[jax.ShapeDtypeStruct((2, n_tok), jnp.int32),
                   jax.ShapeDtypeStruct((1, blk_lanes), jnp.int32),
                   jax.ShapeDtypeStruct((1, 128), jnp.int32)],
        scratch_shapes=[pltpu.VMEM((N_EXPERTS, 1), F32), pltpu.VMEM((N_EXPERTS, 1), F32)],
        compiler_params=pltpu.CompilerParams(dimension_semantics=("arbitrary", "arbitrary")),
        name="route",
    )(eid)


def _row_copy(src_ref, src_row, dst_ref, dst_row, sem):
    return pltpu.make_async_copy(src_ref.at[pl.ds(src_row, 1)], dst_ref.at[pl.ds(dst_row, 1)], sem)


def _dispatch_kernel(dest_ref, hn2_ref, xs_in_ref, xs_ref, sem):
    del xs_in_ref
    rows = hn2_ref.shape[0]

    def issue(r, carry):
        _row_copy(hn2_ref, r, xs_ref, dest_ref[r], sem).start()
        _row_copy(hn2_ref, r, xs_ref, dest_ref[rows + r], sem).start()
        return carry

    lax.fori_loop(0, rows, issue, 0)

    def drain(r, carry):
        _row_copy(hn2_ref, 0, xs_ref, 0, sem).wait()
        _row_copy(hn2_ref, 0, xs_ref, 0, sem).wait()
        return carry

    lax.fori_loop(0, rows, drain, 0)


def _dispatch(dest_tiles, hn2, n_rows_out):
    n_tok, d = hn2.shape
    rows = MOVE_ROWS
    xs0 = jnp.zeros((n_rows_out, d), F32)
    return pl.pallas_call(
        _dispatch_kernel,
        grid=(n_tok // rows,),
        in_specs=[pl.BlockSpec((2 * rows,), lambda i: (i,), memory_space=pltpu.SMEM),
                  pl.BlockSpec((rows, d), lambda i: (i, 0)),
                  pl.BlockSpec(memory_space=pl.ANY)],
        out_specs=pl.BlockSpec(memory_space=pl.ANY),
        out_shape=jax.ShapeDtypeStruct((n_rows_out, d), F32),
        scratch_shapes=[pltpu.SemaphoreType.DMA],
        input_output_aliases={2: 0},
        compiler_params=pltpu.CompilerParams(dimension_semantics=("arbitrary",)),
        name="dispatch",
    )(dest_tiles, hn2, xs0)


def _expert_kernel(blk_ref, meta_ref, xs_ref, wgu_ref, wd_ref, ys_ref):
    j = pl.program_id(0)
    ff = wd_ref.shape[0]

    @pl.when(j < meta_ref[0])
    def _():
        xb = xs_ref[...].astype(BF16)
        gu = jnp.dot(xb, wgu_ref[...], preferred_element_type=F32)
        gp = gu[:, 0:ff]
        act = gp * _sigmoid(gp) * gu[:, ff:2 * ff]
        ys_ref[...] = jnp.dot(act.astype(BF16), wd_ref[...], preferred_element_type=F32)

    @pl.when(j >= meta_ref[0])
    def _():
        ys_ref[...] = jnp.zeros_like(ys_ref)


def _experts(block_e, n_used, xs, w_gu, w_down, n_blocks):
    d = xs.shape[1]
    ff = w_down.shape[1]
    grid_spec = pltpu.PrefetchScalarGridSpec(
        num_scalar_prefetch=2,
        grid=(n_blocks,),
        in_specs=[pl.BlockSpec((MOE_BLOCK, d), lambda j, be, nu: (jnp.minimum(j, nu[0] - 1), 0)),
                  pl.BlockSpec((None, d, 2 * ff), lambda j, be, nu: (be[j], 0, 0)),
                  pl.BlockSpec((None, ff, d), lambda j, be, nu: (be[j], 0, 0))],
        out_specs=pl.BlockSpec((MOE_BLOCK, d), lambda j, be, nu: (j, 0)),
    )
    return pl.pallas_call(
        _expert_kernel,
        grid_spec=grid_spec,
        out_shape=jax.ShapeDtypeStruct(xs.shape, F32),
        compiler_params=pltpu.CompilerParams(dimension_semantics=("arbitrary",),
                                             vmem_limit_bytes=VMEM_LIMIT),
        name="experts",
    )(block_e, n_used, xs, w_gu, w_down)


def _combine_kernel(dest_ref, ys_ref, rw_ref, h1_ref, ada_ref, g_ref, out_ref, buf0, buf1, sem):
    rows = h1_ref.shape[0]

    def issue(r, carry):
        _row_copy(ys_ref, dest_ref[r], buf0, r, sem).start()
        _row_copy(ys_ref, dest_ref[rows + r], buf1, r, sem).start()
        return carry

    lax.fori_loop(0, rows, issue, 0)

    def drain(r, carry):
        _row_copy(ys_ref, 0, buf0, 0, sem).wait()
        _row_copy(ys_ref, 0, buf1, 0, sem).wait()
        return carry

    lax.fori_loop(0, rows, drain, 0)

    rw = rw_ref[...]
    y = rw[:, 0:1] * buf0[...] + rw[:, 1:2] * buf1[...]
    gt2 = ada_ref[5:6, :]
    out_ref[...] = h1_ref[...] + gt2 * _rms_norm(y, g_ref[...])


def _combine(dest_tiles, ys, rw_cols, h1, ada, g_post, seq):
    n_tok, d = h1.shape
    rows = MOVE_ROWS
    per_batch = seq // rows
    return pl.pallas_call(
        _combine_kernel,
        grid=(n_tok // rows,),
        in_specs=[pl.BlockSpec((2 * rows,), lambda i: (i,), memory_space=pltpu.SMEM),
                  pl.BlockSpec(memory_space=pl.ANY),
                  pl.BlockSpec((rows, 2), lambda i: (i, 0)),
                  pl.BlockSpec((rows, d), lambda i: (i, 0)),
                  pl.BlockSpec((None, 6, d), lambda i: (i // per_batch, 0, 0)),
                  pl.BlockSpec((1, d), lambda i: (0, 0))],
        out_specs=pl.BlockSpec((rows, d), lambda i: (i, 0)),
        out_shape=jax.ShapeDtypeStruct((n_tok, d), F32),
        scratch_shapes=[pltpu.VMEM((rows, d), F32), pltpu.VMEM((rows, d), F32), pltpu.SemaphoreType.DMA],
        compiler_params=pltpu.CompilerParams(dimension_semantics=("arbitrary",)),
        name="combine",
    )(dest_tiles, ys, rw_cols, h1, ada, g_post)


def kernel(x, c, w_ada, b_ada, g_pre_mix, g_post_mix, w_in, hg_lb_logits, hg_norm_g, pool_w, pool_scale,
           w_branch_hg, w_branch_pool, w_gate, b_gate, w_out, g_pre_ffn, g_post_ffn, w_router_group,
           b_router_group, w_router_expert, b_router_expert, w_exp_gate, w_exp_up, w_exp_down):
    depth = w_in.shape[0]
    bsz, seq, d = x.shape
    n_tok = bsz * seq
    n_blocks = -(-(n_tok * 2) // MOE_BLOCK) + N_EXPERTS
    assert seq % MIX_ROWS == 0 and MIX_ROWS % CHUNK == 0 and seq % MOVE_ROWS == 0
    assert n_tok % ROUTE_COLS == 0 and hg_lb_logits.shape[0] == 2 and depth == 1

    h = x
    for l in range(depth):
        ada = _ada(c, w_ada[l], b_ada[l]).reshape(bsz, 6, d)
        w1 = jnp.concatenate([w_in[l], w_gate[l]], axis=1).astype(BF16)
        w_router_t = jnp.concatenate(
            [w_router_group[l].T, jnp.zeros((8 - MOE_GROUPS, d), F32), w_router_expert[l].T], axis=0)
        b_router = jnp.concatenate(
            [b_router_group[l], jnp.zeros((8 - MOE_GROUPS,), F32), b_router_expert[l]])[:, None]
        h1, hn2, eid, rw = _mixer(
            h, ada, g_pre_mix[l][None], g_post_mix[l][None], g_pre_ffn[l][None], w1, b_gate[l][None],
            hg_lb_logits, hg_norm_g[l][None], pool_w[l].astype(BF16), pool_scale[l][None],
            w_branch_hg[l].astype(BF16), w_branch_pool[l].astype(BF16), w_out[l].astype(BF16),
            w_router_t, b_router)

        dest, block_e, meta = _route(eid, n_blocks)
        n_mt = n_tok // MOVE_ROWS
        dest_tiles = dest.reshape(2, n_mt, MOVE_ROWS).transpose(1, 0, 2).reshape(-1)
        xs = _dispatch(dest_tiles, hn2.reshape(n_tok, d), n_blocks * MOE_BLOCK)
        w_gu = jnp.concatenate([w_exp_gate[l], w_exp_up[l]], axis=2).astype(BF16)
        ys = _experts(block_e[0, :n_blocks], meta[0, :1], xs, w_gu, w_exp_down[l].astype(BF16), n_blocks)
        h = _combine(dest_tiles, ys, rw.T, h1.reshape(n_tok, d), ada, g_post_ffn[l][None], seq)
        h = h.reshape(bsz, seq, d)
    return h
```

```python
import functools

import jax
import jax.numpy as jnp
from jax import lax
from jax.experimental import pallas as pl
from jax.experimental.pallas import tpu as pltpu

F32 = jnp.float32
BF16 = jnp.bfloat16
HIGHEST = lax.Precision.HIGHEST

CHUNK = 64
HG_HEADS = 4
POOL_WINDOWS = (2, 4, 8, 16)
POOL_GROUPS = 4
MOE_GROUPS = 4
MOE_EPG = 8
N_EXPERTS = MOE_GROUPS * MOE_EPG
MOE_BLOCK = 256
EPS = 1e-6

MIX_ROWS = 256
POOL_HALO = 128
ROUTE_COLS = 512
MOVE_ROWS = 512
MOVE_UNROLL = 8
LOG_DECAY_FLOOR = -80.0
VMEM_LIMIT = 56 * 1024 * 1024

NT_DIMS = (((1,), (1,)), ((), ()))
TN_DIMS = (((0,), (0,)), ((), ()))


def _sigmoid(v):
    return 1.0 / (1.0 + jnp.exp(-v))


def _rms_norm(v, g):
    return v * lax.rsqrt(jnp.mean(v * v, axis=-1, keepdims=True) + EPS) * g


def _ada_kernel(c_ref, w_ref, b_ref, o_ref):
    c = c_ref[...]
    cond = c * _sigmoid(c)
    o_ref[...] = jnp.dot(cond, w_ref[...], preferred_element_type=F32, precision=HIGHEST) + b_ref[...]


def _ada(c, w_ada, b_ada):
    bsz, d = c.shape
    n_out = w_ada.shape[1]
    return pl.pallas_call(
        _ada_kernel,
        grid=(n_out // d,),
        in_specs=[pl.BlockSpec((bsz, d), lambda i: (0, 0)),
                  pl.BlockSpec((d, d), lambda i: (0, i)),
                  pl.BlockSpec((1, d), lambda i: (0, i))],
        out_specs=pl.BlockSpec((bsz, d), lambda i: (0, i)),
        out_shape=jax.ShapeDtypeStruct((bsz, n_out), F32),
        name="ada",
    )(c, w_ada, b_ada.reshape(1, n_out))


def _mix_kernel(x_ref, ada_ref, gpre_ref, gpost_ref, gffn_ref, w1_ref, bgate_ref, lbl_ref, hgn_ref,
                tril_ref, band_ref, poolw_ref, pools_ref, wbh_ref, wbp_ref, wout_ref, wr_ref, br_ref,
                h1_ref, hn2_ref, eid_ref, rw_ref,
                st_ref, ext_ref, o_scr):
    t_rows, d = x_ref.shape
    hgw = hgn_ref.shape[1]
    hd = hgw // HG_HEADS
    pw = pools_ref.shape[1]
    gd = pw // POOL_GROUPS
    j = pl.program_id(1)

    @pl.when(j == 0)
    def _():
        st_ref[...] = jnp.zeros_like(st_ref)
        ext_ref[t_rows:t_rows + POOL_HALO, :] = jnp.zeros((POOL_HALO, pw), BF16)

    x = x_ref[...]
    ada = ada_ref[...]
    sh1, sc1, gt1 = ada[0:1], ada[1:2], ada[2:3]
    sh2, sc2 = ada[3:4], ada[4:5]

    hn = _rms_norm(x, gpre_ref[...]) * (1.0 + sc1) + sh1
    hb = hn.astype(BF16)
    c_u = 4 * hgw
    c_gate = c_u + pw
    qfig = jnp.dot(hb, w1_ref[:, 0:c_u], preferred_element_type=F32)
    u = jnp.dot(hb, w1_ref[:, c_u:c_gate], preferred_element_type=F32)
    gl = jnp.dot(hb, w1_ref[:, c_gate:], preferred_element_type=F32) + bgate_ref[...]

    qr, fr = qfig[:, 0:hgw], qfig[:, hgw:2 * hgw]
    vr, gr = qfig[:, 2 * hgw:3 * hgw], qfig[:, 3 * hgw:4 * hgw]
    lbl = lbl_ref[...]
    lmax = jnp.maximum(lbl[0:1], lbl[1:2])
    e0 = jnp.exp(lbl[0:1] - lmax)
    lb = e0 / (e0 + jnp.exp(lbl[1:2] - lmax))
    q = qr * _sigmoid(qr)
    f = lb + (1.0 - lb) * _sigmoid(fr)
    k = 1.0 - f
    lf = jnp.log(f)
    lf_hi = lf.astype(BF16)
    r1 = lf - lf_hi.astype(F32)
    lf_mid = r1.astype(BF16)
    lf_lo = (r1 - lf_mid.astype(F32)).astype(BF16)
    tril = tril_ref[...]
    b = (jnp.dot(tril, lf_hi, preferred_element_type=F32)
         + jnp.dot(tril, lf_mid, preferred_element_type=F32)
         + jnp.dot(tril, lf_lo, preferred_element_type=F32))
    bc = jnp.maximum(b, LOG_DECAY_FLOOR)
    qt = (q * jnp.exp(bc)).astype(BF16)
    kt = (k * jnp.exp(-bc)).astype(BF16)
    vb = vr.astype(BF16)
    og = hgn_ref[...] * (gr * _sigmoid(gr))

    row = lax.broadcasted_iota(jnp.int32, (CHUNK, CHUNK), 0)
    col = lax.broadcasted_iota(jnp.int32, (CHUNK, CHUNK), 1)
    causal = row >= col
    for ci in range(t_rows // CHUNK):
        r0 = ci * CHUNK
        for h in range(HG_HEADS):
            c0 = h * hd
            qt_c = qt[r0:r0 + CHUNK, c0:c0 + hd]
            kt_c = kt[r0:r0 + CHUNK, c0:c0 + hd]
            v_c = vb[r0:r0 + CHUNK, c0:c0 + hd]
            b_c = b[r0:r0 + CHUNK, c0:c0 + hd]
            k_c = k[r0:r0 + CHUNK, c0:c0 + hd]
            b_last = b_c[CHUNK - 1:CHUNK, :]
            st_t = st_ref[h]
            s = lax.dot_general(qt_c, kt_c, NT_DIMS, preferred_element_type=F32)
            s = jnp.where(causal, s, 0.0)
            o = (jnp.dot(s.astype(BF16), v_c, preferred_element_type=F32)
                 + lax.dot_general(qt_c, st_t.astype(BF16), NT_DIMS, preferred_element_type=F32))
            kh = (k_c * jnp.exp(b_last - b_c)).astype(BF16)
            upd = lax.dot_general(v_c, kh, TN_DIMS, preferred_element_type=F32)
            st_ref[h] = jnp.exp(b_last) * st_t + upd
            o = o * lax.rsqrt(jnp.mean(o * o, axis=-1, keepdims=True) + EPS)
            o_scr[r0:r0 + CHUNK, c0:c0 + hd] = (o * og[r0:r0 + CHUNK, c0:c0 + hd]).astype(BF16)
    y_hg = jnp.dot(o_scr[...], wbh_ref[...], preferred_element_type=F32)

    ext_ref[0:POOL_HALO, :] = ext_ref[t_rows:t_rows + POOL_HALO, :]
    ext_ref[POOL_HALO:POOL_HALO + t_rows, :] = u.astype(BF16)
    pos = j * t_rows + lax.broadcasted_iota(jnp.int32, (t_rows, gd), 0)
    mixed = []
    for g in range(POOL_GROUPS):
        c0 = g * gd
        wsum = jnp.dot(band_ref[g], ext_ref[:, c0:c0 + gd], preferred_element_type=F32)
        cnt = jnp.minimum(pos + 1, POOL_WINDOWS[g]).astype(F32)
        pooled = wsum / cnt - u[:, c0:c0 + gd]
        mixed.append(jnp.dot(pooled.astype(BF16), poolw_ref[g], preferred_element_type=F32))
    mixed = jnp.concatenate(mixed, axis=1) * pools_ref[...]
    y_pool = jnp.dot(mixed.astype(BF16), wbp_ref[...], preferred_element_type=F32)

    gate = _sigmoid(gl)
    merged = gate[:, 0:d] * y_hg + gate[:, d:2 * d] * y_pool
    mix = jnp.dot(merged.astype(BF16), wout_ref[...], preferred_element_type=F32)
    h1 = x + gt1 * _rms_norm(mix, gpost_ref[...])
    h1_ref[...] = h1

    hn2 = _rms_norm(h1, gffn_ref[...]) * (1.0 + sc2) + sh2
    hn2_ref[...] = hn2
    lt = lax.dot_general(wr_ref[...], hn2, NT_DIMS, preferred_element_type=F32, precision=HIGHEST)
    lt = lt + br_ref[...]
    lg = lt[0:MOE_GROUPS]
    gmax = jnp.max(lg, axis=0, keepdims=True)
    p_g = 1.0 / jnp.sum(jnp.exp(lg - gmax), axis=0, keepdims=True)
    gi = lax.broadcasted_iota(jnp.int32, lg.shape, 0).astype(F32)
    g_idx = jnp.min(jnp.where(lg == gmax, gi, float(MOE_GROUPS)), axis=0, keepdims=True)
    le = lt[8:8 + MOE_EPG]
    for g in range(1, MOE_GROUPS):
        le = jnp.where(g_idx == float(g), lt[8 + g * MOE_EPG:8 + (g + 1) * MOE_EPG], le)
    ei = lax.broadcasted_iota(jnp.int32, le.shape, 0).astype(F32)
    m1 = jnp.max(le, axis=0, keepdims=True)
    i1 = jnp.min(jnp.where(le == m1, ei, float(MOE_EPG)), axis=0, keepdims=True)
    le2 = jnp.where(ei == i1, -jnp.inf, le)
    m2 = jnp.max(le2, axis=0, keepdims=True)
    i2 = jnp.min(jnp.where(le2 == m2, ei, float(MOE_EPG)), axis=0, keepdims=True)
    r = jnp.exp(m2 - m1)
    w_first = p_g / (1.0 + r)
    eid_ref[0:1, :] = (g_idx * MOE_EPG + i1).astype(jnp.int32)
    eid_ref[1:2, :] = (g_idx * MOE_EPG + i2).astype(jnp.int32)
    rw_ref[0:1, :] = w_first
    rw_ref[1:2, :] = w_first * r


def _band_matrices(t_rows):
    t = jnp.arange(t_rows)[:, None] + POOL_HALO
    jx = jnp.arange(t_rows + POOL_HALO)[None, :]
    return jnp.stack([((jx <= t) & (jx > t - w)) for w in POOL_WINDOWS]).astype(BF16)


def _chunk_tril(t_rows):
    r = jnp.arange(t_rows)[:, None]
    c = jnp.arange(t_rows)[None, :]
    return ((r >= c) & (r // CHUNK == c // CHUNK)).astype(BF16)


def _mixer(x, ada, g_pre, g_post, g_ffn, w1, b_gate, lb_logits, hg_norm_g, pool_w, pool_scale,
           w_bh, w_bp, w_out, w_router_t, b_router):
    bsz, seq, d = x.shape
    t_rows = MIX_ROWS
    n_t = seq // t_rows
    n_tok = bsz * seq
    hgw = hg_norm_g.shape[1]
    pw = pool_scale.shape[1]
    n_r = w_router_t.shape[0]

    def const(shape):
        return pl.BlockSpec(shape, lambda b, j: (0,) * len(shape), pipeline_mode=pl.Buffered(1))

    in_specs = [
        pl.BlockSpec((None, t_rows, d), lambda b, j: (b, j, 0)),
        pl.BlockSpec((None, 6, d), lambda b, j: (b, 0, 0)),
        const((1, d)), const((1, d)), const((1, d)),
        const(w1.shape), const((1, 2 * d)),
        const(lb_logits.shape), const((1, hgw)),
        const((t_rows, t_rows)), const((POOL_GROUPS, t_rows, t_rows + POOL_HALO)),
        const(pool_w.shape), const((1, pw)),
        const(w_bh.shape), const(w_bp.shape), const(w_out.shape),
        const(w_router_t.shape), const((n_r, 1)),
    ]
    out_specs = [
        pl.BlockSpec((None, t_rows, d), lambda b, j: (b, j, 0)),
        pl.BlockSpec((None, t_rows, d), lambda b, j: (b, j, 0)),
        pl.BlockSpec((2, t_rows), lambda b, j: (0, b * n_t + j)),
        pl.BlockSpec((2, t_rows), lambda b, j: (0, b * n_t + j)),
    ]
    out_shape = [
        jax.ShapeDtypeStruct((bsz, seq, d), F32),
        jax.ShapeDtypeStruct((bsz, seq, d), F32),
        jax.ShapeDtypeStruct((2, n_tok), jnp.int32),
        jax.ShapeDtypeStruct((2, n_tok), F32),
    ]
    scratch = [
        pltpu.VMEM((HG_HEADS, hgw // HG_HEADS, hgw // HG_HEADS), F32),
        pltpu.VMEM((t_rows + POOL_HALO, pw), BF16),
        pltpu.VMEM((t_rows, hgw), BF16),
    ]
    return pl.pallas_call(
        _mix_kernel,
        grid=(bsz, n_t),
        in_specs=in_specs, out_specs=out_specs, out_shape=out_shape, scratch_shapes=scratch,
        compiler_params=pltpu.CompilerParams(dimension_semantics=("arbitrary", "arbitrary"),
                                             vmem_limit_bytes=VMEM_LIMIT),
        name="mixer",
    )(x, ada, g_pre, g_post, g_ffn, w1, b_gate, lb_logits, hg_norm_g, _chunk_tril(t_rows),
      _band_matrices(t_rows), pool_w, pool_scale, w_bh, w_bp, w_out, w_router_t, b_router)


def _route_kernel(eid_ref, dest_ref, blk_ref, meta_ref, cnt_ref, run_ref):
    phase = pl.program_id(0)
    i = pl.program_id(1)
    cols = eid_ref.shape[1]
    eidx = lax.broadcasted_iota(jnp.int32, (N_EXPERTS, cols), 0)
    hot0 = eidx == eid_ref[0:1, :]
    hot1 = eidx == eid_ref[1:2, :]
    both = jnp.where(hot0 | hot1, 1.0, 0.0)
    tile_cnt = jnp.sum(both, axis=1, keepdims=True)

    @pl.when((phase == 0) & (i == 0))
    def _():
        cnt_ref[...] = jnp.zeros_like(cnt_ref)

    @pl.when(phase == 0)
    def _():
        cnt_ref[...] += tile_cnt

    @pl.when((phase == 1) & (i == 0))
    def _():
        nblk = jnp.floor((cnt_ref[...] + float(MOE_BLOCK - 1)) * (1.0 / MOE_BLOCK))
        nblk_f = jnp.broadcast_to(nblk, (N_EXPERTS, 128))
        er = lax.broadcasted_iota(jnp.int32, (N_EXPERTS, N_EXPERTS), 0)
        ec = lax.broadcasted_iota(jnp.int32, (N_EXPERTS, N_EXPERTS), 1)
        lower = jnp.where(ec < er, 1.0, 0.0).astype(BF16)
        start_blk = jnp.dot(lower, nblk_f.astype(BF16), preferred_element_type=F32)
        run_ref[...] = start_blk[:, 0:1] * float(MOE_BLOCK)
        end_blk = start_blk + nblk_f
        lane = lax.broadcasted_iota(jnp.int32, (N_EXPERTS, blk_ref.shape[1]), 1).astype(F32)
        done = jnp.where(end_blk[:, 0:1] <= lane, 1.0, 0.0)
        blk_ref[...] = jnp.minimum(jnp.sum(done, axis=0, keepdims=True),
                                   float(N_EXPERTS - 1)).astype(jnp.int32)
        meta_ref[...] = jnp.broadcast_to(end_blk[N_EXPERTS - 1:N_EXPERTS, 0:1],
                                         meta_ref.shape).astype(jnp.int32)

    @pl.when(phase == 1)
    def _():
        r = lax.broadcasted_iota(jnp.int32, (cols, cols), 0)
        c = lax.broadcasted_iota(jnp.int32, (cols, cols), 1)
        before = jnp.where(r < c, 1.0, 0.0).astype(BF16)
        prefix = jnp.dot(both.astype(BF16), before, preferred_element_type=F32)
        slot = run_ref[...] + prefix
        dest_ref[0:1, :] = jnp.sum(jnp.where(hot0, slot, 0.0), axis=0, keepdims=True).astype(jnp.int32)
        dest_ref[1:2, :] = jnp.sum(jnp.where(hot1, slot, 0.0), axis=0, keepdims=True).astype(jnp.int32)
        run_ref[...] += tile_cnt


def _route(eid, n_blocks):
    n_tok = eid.shape[1]
    cols = ROUTE_COLS
    blk_lanes = pl.cdiv(n_blocks, 128) * 128
    return pl.pallas_call(
        _route_kernel,
        grid=(2, n_tok // cols),
        in_specs=[pl.BlockSpec((2, cols), lambda p, i: (0, i))],
        out_specs=[pl.BlockSpec((2, cols), lambda p, i: (0, i * p)),
                   pl.BlockSpec((1, blk_lanes), lambda p, i: (0, 0)),
                   pl.BlockSpec((1, 128), lambda p, i: (0, 0))],
        out_shape=[jax.ShapeDtypeStruct((2, n_tok), jnp.int32),
                   jax.ShapeDtypeStruct((1, blk_lanes), jnp.int32),
                   jax.ShapeDtypeStruct((1, 128), jnp.int32)],
        scratch_shapes=[pltpu.VMEM((N_EXPERTS, 1), F32), pltpu.VMEM((N_EXPERTS, 1), F32)],
        compiler_params=pltpu.CompilerParams(dimension_semantics=("arbitrary", "arbitrary")),
        name="route",
    )(eid)


def _row_copy(src_ref, src_row, dst_ref, dst_row, sem):
    return pltpu.make_async_copy(src_ref.at[pl.ds(src_row, 1)], dst_ref.at[pl.ds(dst_row, 1)], sem)


def _drain_rows(rows, wait_pair):
    def drain(g, carry):
        for _ in range(MOVE_UNROLL):
            wait_pair()
        return carry

    lax.fori_loop(0, rows // MOVE_UNROLL, drain, 0)


def _dispatch_kernel(dest_ref, hn2_ref, xs_in_ref, xs_ref, sem):
    del xs_in_ref
    rows = hn2_ref.shape[0]

    def issue(g, carry):
        base = pl.multiple_of(g * MOVE_UNROLL, MOVE_UNROLL)
        group = hn2_ref.at[pl.ds(base, MOVE_UNROLL)]
        for u in range(MOVE_UNROLL):
            _row_copy(group, u, xs_ref, dest_ref[base + u], sem).start(priority=0)
            _row_copy(group, u, xs_ref, dest_ref[rows + base + u], sem).start(priority=1)
        return carry

    lax.fori_loop(0, rows // MOVE_UNROLL, issue, 0)

    def wait_pair():
        _row_copy(hn2_ref, 0, xs_ref, 0, sem).wait()
        _row_copy(hn2_ref, 0, xs_ref, 0, sem).wait()

    _drain_rows(rows, wait_pair)


def _dispatch(dest_tiles, hn2, n_rows_out):
    n_tok, d = hn2.shape
    rows = MOVE_ROWS
    xs0 = jnp.zeros((n_rows_out, d), F32)
    return pl.pallas_call(
        _dispatch_kernel,
        grid=(n_tok // rows,),
        in_specs=[pl.BlockSpec((2 * rows,), lambda i: (i,), memory_space=pltpu.SMEM),
                  pl.BlockSpec((rows, d), lambda i: (i, 0)),
                  pl.BlockSpec(memory_space=pl.ANY)],
        out_specs=pl.BlockSpec(memory_space=pl.ANY),
        out_shape=jax.ShapeDtypeStruct((n_rows_out, d), F32),
        scratch_shapes=[pltpu.SemaphoreType.DMA],
        input_output_aliases={2: 0},
        compiler_params=pltpu.CompilerParams(dimension_semantics=("arbitrary",)),
        name="dispatch",
    )(dest_tiles, hn2, xs0)


def _expert_kernel(blk_ref, meta_ref, xs_ref, wg_ref, wu_ref, wd_ref, ys_ref, wg_s, wu_s, wd_s):
    j = pl.program_id(0)
    used = j < meta_ref[0]
    first_of_expert = (j == 0) | (blk_ref[j] != blk_ref[jnp.maximum(j - 1, 0)])

    @pl.when(used & first_of_expert)
    def _():
        wg_s[...] = wg_ref[...].astype(BF16)
        wu_s[...] = wu_ref[...].astype(BF16)
        wd_s[...] = wd_ref[...].astype(BF16)

    @pl.when(used)
    def _():
        xb = xs_ref[...].astype(BF16)
        gp = jnp.dot(xb, wg_s[...], preferred_element_type=F32)
        up = jnp.dot(xb, wu_s[...], preferred_element_type=F32)
        act = gp * _sigmoid(gp) * up
        ys_ref[...] = jnp.dot(act.astype(BF16), wd_s[...], preferred_element_type=F32)

    @pl.when(jnp.logical_not(used))
    def _():
        ys_ref[...] = jnp.zeros_like(ys_ref)


def _experts(block_e, n_used, xs, w_gate, w_up, w_down, n_blocks):
    d = xs.shape[1]
    ff = w_down.shape[1]

    def row_block(j, be, nu):
        return (jnp.minimum(j, nu[0] - 1), 0)

    def expert_block(j, be, nu):
        return (be[jnp.minimum(j, nu[0] - 1)], 0, 0)

    grid_spec = pltpu.PrefetchScalarGridSpec(
        num_scalar_prefetch=2,
        grid=(n_blocks,),
        in_specs=[pl.BlockSpec((MOE_BLOCK, d), row_block),
                  pl.BlockSpec((None, d, ff), expert_block),
                  pl.BlockSpec((None, d, ff), expert_block),
                  pl.BlockSpec((None, ff, d), expert_block)],
        out_specs=pl.BlockSpec((MOE_BLOCK, d), lambda j, be, nu: (j, 0)),
        scratch_shapes=[pltpu.VMEM((d, ff), BF16), pltpu.VMEM((d, ff), BF16), pltpu.VMEM((ff, d), BF16)],
    )
    return pl.pallas_call(
        _expert_kernel,
        grid_spec=grid_spec,
        out_shape=jax.ShapeDtypeStruct(xs.shape, F32),
        compiler_params=pltpu.CompilerParams(dimension_semantics=("arbitrary",),
                                             vmem_limit_bytes=VMEM_LIMIT),
        name="experts",
    )(block_e, n_used, xs, w_gate, w_up, w_down)


def _combine_kernel(dest_ref, ys_ref, rw_ref, h1_ref, ada_ref, g_ref, out_ref, buf0, buf1, sem):
    rows = h1_ref.shape[0]

    def issue(g, carry):
        base = pl.multiple_of(g * MOVE_UNROLL, MOVE_UNROLL)
        group0 = buf0.at[pl.ds(base, MOVE_UNROLL)]
        group1 = buf1.at[pl.ds(base, MOVE_UNROLL)]
        for u in range(MOVE_UNROLL):
            _row_copy(ys_ref, dest_ref[base + u], group0, u, sem).start(priority=0)
            _row_copy(ys_ref, dest_ref[rows + base + u], group1, u, sem).start(priority=1)
        return carry

    lax.fori_loop(0, rows // MOVE_UNROLL, issue, 0)

    def wait_pair():
        _row_copy(ys_ref, 0, buf0, 0, sem).wait()
        _row_copy(ys_ref, 0, buf1, 0, sem).wait()

    _drain_rows(rows, wait_pair)

    rw = rw_ref[...]
    y = rw[:, 0:1] * buf0[...] + rw[:, 1:2] * buf1[...]
    gt2 = ada_ref[5:6, :]
    out_ref[...] = h1_ref[...] + gt2 * _rms_norm(y, g_ref[...])


def _combine(dest_tiles, ys, rw_cols, h1, ada, g_post, seq):
    n_tok, d = h1.shape
    rows = MOVE_ROWS
    per_batch = seq // rows
    return pl.pallas_call(
        _combine_kernel,
        grid=(n_tok // rows,),
        in_specs=[pl.BlockSpec((2 * rows,), lambda i: (i,), memory_space=pltpu.SMEM),
                  pl.BlockSpec(memory_space=pl.ANY),
                  pl.BlockSpec((rows, 2), lambda i: (i, 0)),
                  pl.BlockSpec((rows, d), lambda i: (i, 0)),
                  pl.BlockSpec((None, 6, d), lambda i: (i // per_batch, 0, 0)),
                  pl.BlockSpec((1, d), lambda i: (0, 0))],
        out_specs=pl.BlockSpec((rows, d), lambda i: (i, 0)),
        out_shape=jax.ShapeDtypeStruct((n_tok, d), F32),
        scratch_shapes=[pltpu.VMEM((rows, d), F32), pltpu.VMEM((rows, d), F32), pltpu.SemaphoreType.DMA],
        compiler_params=pltpu.CompilerParams(dimension_semantics=("arbitrary",)),
        name="combine",
    )(dest_tiles, ys, rw_cols, h1, ada, g_post)


def kernel(x, c, w_ada, b_ada, g_pre_mix, g_post_mix, w_in, hg_lb_logits, hg_norm_g, pool_w, pool_scale,
           w_branch_hg, w_branch_pool, w_gate, b_gate, w_out, g_pre_ffn, g_post_ffn, w_router_group,
           b_router_group, w_router_expert, b_router_expert, w_exp_gate, w_exp_up, w_exp_down):
    depth = w_in.shape[0]
    bsz, seq, d = x.shape
    n_tok = bsz * seq
    n_blocks = -(-(n_tok * 2) // MOE_BLOCK) + N_EXPERTS
    assert seq % MIX_ROWS == 0 and MIX_ROWS % CHUNK == 0 and seq % MOVE_ROWS == 0
    assert n_tok % ROUTE_COLS == 0 and hg_lb_logits.shape[0] == 2 and depth == 1

    h = x
    for l in range(depth):
        ada = _ada(c, w_ada[l], b_ada[l]).reshape(bsz, 6, d)
        w1 = jnp.concatenate([w_in[l], w_gate[l]], axis=1).astype(BF16)
        w_router_t = jnp.concatenate(
            [w_router_group[l].T, jnp.zeros((8 - MOE_GROUPS, d), F32), w_router_expert[l].T], axis=0)
        b_router = jnp.concatenate(
            [b_router_group[l], jnp.zeros((8 - MOE_GROUPS,), F32), b_router_expert[l]])[:, None]
        h1, hn2, eid, rw = _mixer(
            h, ada, g_pre_mix[l][None], g_post_mix[l][None], g_pre_ffn[l][None], w1, b_gate[l][None],
            hg_lb_logits, hg_norm_g[l][None], pool_w[l].astype(BF16), pool_scale[l][None],
            w_branch_hg[l].astype(BF16), w_branch_pool[l].astype(BF16), w_out[l].astype(BF16),
            w_router_t, b_router)

        dest, block_e, meta = _route(eid, n_blocks)
        n_mt = n_tok // MOVE_ROWS
        dest_tiles = dest.reshape(2, n_mt, MOVE_ROWS).transpose(1, 0, 2).reshape(-1)
        xs = _dispatch(dest_tiles, hn2.reshape(n_tok, d), n_blocks * MOE_BLOCK)
        ys = _experts(block_e[0, :n_blocks], meta[0, :1], xs, w_exp_gate[l], w_exp_up[l], w_exp_down[l],
                      n_blocks)
        h = _combine(dest_tiles, ys, rw.T, h1.reshape(n_tok, d), ada, g_post_ffn[l][None], seq)
        h = h.reshape(bsz, seq, d)
    return h
```

```python
import functools

import jax
import jax.numpy as jnp
from jax import lax
from jax.experimental import pallas as pl
from jax.experimental.pallas import tpu as pltpu

F32 = jnp.float32
BF16 = jnp.bfloat16
HIGHEST = lax.Precision.HIGHEST

CHUNK = 64
HG_HEADS = 4
POOL_WINDOWS = (2, 4, 8, 16)
POOL_GROUPS = 4
MOE_GROUPS = 4
MOE_EPG = 8
N_EXPERTS = MOE_GROUPS * MOE_EPG
MOE_BLOCK = 256
EPS = 1e-6

MIX_ROWS = 256
POOL_HALO = 128
ROUTE_COLS = 512
MOVE_ROWS = 512
MOVE_UNROLL = 8
ROUTER_COLS = 128
LOG_DECAY_FLOOR = -80.0
VMEM_LIMIT = 56 * 1024 * 1024

NT_DIMS = (((1,), (1,)), ((), ()))
TN_DIMS = (((0,), (0,)), ((), ()))


def _sigmoid(v):
    return 1.0 / (1.0 + jnp.exp(-v))


def _rms_norm(v, g):
    return v * lax.rsqrt(jnp.mean(v * v, axis=-1, keepdims=True) + EPS) * g


def _pack_bf16_pairs(v):
    n = v.shape[1] // 2
    lo = lax.bitcast_convert_type(v[:, :n].astype(BF16).astype(F32), jnp.uint32)
    hi = lax.bitcast_convert_type(v[:, n:].astype(BF16).astype(F32), jnp.uint32)
    return hi | (lo >> 16)


def _unpack_bf16_pairs(p):
    lo = lax.bitcast_convert_type(p << 16, F32)
    hi = lax.bitcast_convert_type(p & jnp.uint32(0xFFFF0000), F32)
    return jnp.concatenate([lo, hi], axis=1)


def _ada_kernel(c_ref, w_ref, b_ref, o_ref):
    c = c_ref[...]
    cond = c * _sigmoid(c)
    o_ref[...] = jnp.dot(cond, w_ref[...], preferred_element_type=F32, precision=HIGHEST) + b_ref[...]


def _ada(c, w_ada, b_ada):
    bsz, d = c.shape
    n_out = w_ada.shape[1]
    return pl.pallas_call(
        _ada_kernel,
        grid=(n_out // d,),
        in_specs=[pl.BlockSpec((bsz, d), lambda i: (0, 0)),
                  pl.BlockSpec((d, d), lambda i: (0, i)),
                  pl.BlockSpec((1, d), lambda i: (0, i))],
        out_specs=pl.BlockSpec((bsz, d), lambda i: (0, i)),
        out_shape=jax.ShapeDtypeStruct((bsz, n_out), F32),
        name="ada",
    )(c, w_ada, b_ada.reshape(1, n_out))


def _mix_kernel(x_ref, ada_ref, gpre_ref, gpost_ref, gffn_ref, w1_ref, bgate_ref, lbl_ref, hgn_ref,
                tril_ref, band_ref, poolw_ref, pools_ref, wbh_ref, wbp_ref, wout_ref, wr_ref, br_ref,
                h1_ref, hn2_ref, eid_ref, rw_ref,
                st_ref, ext_ref, o_scr):
    t_rows, d = x_ref.shape
    hgw = hgn_ref.shape[1]
    hd = hgw // HG_HEADS
    pw = pools_ref.shape[1]
    gd = pw // POOL_GROUPS
    j = pl.program_id(1)

    @pl.when(j == 0)
    def _():
        st_ref[...] = jnp.zeros_like(st_ref)
        ext_ref[t_rows:t_rows + POOL_HALO, :] = jnp.zeros((POOL_HALO, pw), BF16)

    x = x_ref[...]
    ada = ada_ref[...]
    sh1, sc1, gt1 = ada[0:1], ada[1:2], ada[2:3]
    sh2, sc2 = ada[3:4], ada[4:5]

    hn = _rms_norm(x, gpre_ref[...]) * (1.0 + sc1) + sh1
    hb = hn.astype(BF16)
    c_u = 4 * hgw
    c_gate = c_u + pw
    proj = jnp.dot(hb, w1_ref[...], preferred_element_type=F32)
    qfig = proj[:, 0:c_u]
    u = proj[:, c_u:c_gate]
    gl = proj[:, c_gate:] + bgate_ref[...]

    qr, fr = qfig[:, 0:hgw], qfig[:, hgw:2 * hgw]
    vr, gr = qfig[:, 2 * hgw:3 * hgw], qfig[:, 3 * hgw:4 * hgw]
    lbl = lbl_ref[...]
    lmax = jnp.maximum(lbl[0:1], lbl[1:2])
    e0 = jnp.exp(lbl[0:1] - lmax)
    lb = e0 / (e0 + jnp.exp(lbl[1:2] - lmax))
    q = qr * _sigmoid(qr)
    f = lb + (1.0 - lb) * _sigmoid(fr)
    k = 1.0 - f
    lf = jnp.log(f)
    lf_hi = lf.astype(BF16)
    r1 = lf - lf_hi.astype(F32)
    lf_mid = r1.astype(BF16)
    lf_lo = (r1 - lf_mid.astype(F32)).astype(BF16)
    tril = tril_ref[...]
    b = (jnp.dot(tril, lf_hi, preferred_element_type=F32)
         + jnp.dot(tril, lf_mid, preferred_element_type=F32)
         + jnp.dot(tril, lf_lo, preferred_element_type=F32))
    bc = jnp.maximum(b, LOG_DECAY_FLOOR)
    qt = (q * jnp.exp(bc)).astype(BF16)
    kt = (k * jnp.exp(-bc)).astype(BF16)
    vb = vr.astype(BF16)
    og = hgn_ref[...] * (gr * _sigmoid(gr))

    row = lax.broadcasted_iota(jnp.int32, (CHUNK, CHUNK), 0)
    col = lax.broadcasted_iota(jnp.int32, (CHUNK, CHUNK), 1)
    causal = row >= col
    for ci in range(t_rows // CHUNK):
        r0 = ci * CHUNK
        for h in range(HG_HEADS):
            c0 = h * hd
            qt_c = qt[r0:r0 + CHUNK, c0:c0 + hd]
            kt_c = kt[r0:r0 + CHUNK, c0:c0 + hd]
            v_c = vb[r0:r0 + CHUNK, c0:c0 + hd]
            b_c = b[r0:r0 + CHUNK, c0:c0 + hd]
            k_c = k[r0:r0 + CHUNK, c0:c0 + hd]
            b_last = b_c[CHUNK - 1:CHUNK, :]
            st_t = st_ref[h]
            s = lax.dot_general(qt_c, kt_c, NT_DIMS, preferred_element_type=F32)
            s = jnp.where(causal, s, 0.0)
            o = (jnp.dot(s.astype(BF16), v_c, preferred_element_type=F32)
                 + lax.dot_general(qt_c, st_t.astype(BF16), NT_DIMS, preferred_element_type=F32))
            kh = (k_c * jnp.exp(b_last - b_c)).astype(BF16)
            upd = lax.dot_general(v_c, kh, TN_DIMS, preferred_element_type=F32)
            st_ref[h] = jnp.exp(b_last) * st_t + upd
            o = o * lax.rsqrt(jnp.mean(o * o, axis=-1, keepdims=True) + EPS)
            o_scr[r0:r0 + CHUNK, c0:c0 + hd] = (o * og[r0:r0 + CHUNK, c0:c0 + hd]).astype(BF16)
    y_hg = jnp.dot(o_scr[...], wbh_ref[...], preferred_element_type=F32)

    ext_ref[0:POOL_HALO, :] = ext_ref[t_rows:t_rows + POOL_HALO, :]
    ext_ref[POOL_HALO:POOL_HALO + t_rows, :] = u.astype(BF16)
    pos = j * t_rows + lax.broadcasted_iota(jnp.int32, (t_rows, gd), 0)
    mixed = []
    for g in range(POOL_GROUPS):
        c0 = g * gd
        wsum = jnp.dot(band_ref[g], ext_ref[:, c0:c0 + gd], preferred_element_type=F32)
        cnt = jnp.minimum(pos + 1, POOL_WINDOWS[g]).astype(F32)
        pooled = wsum / cnt - u[:, c0:c0 + gd]
        mixed.append(jnp.dot(pooled.astype(BF16), poolw_ref[g], preferred_element_type=F32))
    mixed = jnp.concatenate(mixed, axis=1) * pools_ref[...]
    y_pool = jnp.dot(mixed.astype(BF16), wbp_ref[...], preferred_element_type=F32)

    gate = _sigmoid(gl)
    merged = gate[:, 0:d] * y_hg + gate[:, d:2 * d] * y_pool
    mix = jnp.dot(merged.astype(BF16), wout_ref[...], preferred_element_type=F32)
    h1 = x + gt1 * _rms_norm(mix, gpost_ref[...])
    h1_ref[...] = h1

    hn2 = _rms_norm(h1, gffn_ref[...]) * (1.0 + sc2) + sh2
    hn2_ref[...] = _pack_bf16_pairs(hn2)
    hn2_hi = hn2.astype(BF16)
    hn2_lo = (hn2 - hn2_hi.astype(F32)).astype(BF16)
    prod = jnp.dot(jnp.concatenate([hn2_hi, hn2_lo], axis=0), wr_ref[...], preferred_element_type=F32)
    n_r = wr_ref.shape[1] // 2
    logits = prod[0:t_rows, 0:n_r] + prod[t_rows:2 * t_rows, 0:n_r] + prod[0:t_rows, n_r:2 * n_r]
    lt = logits.T + br_ref[...]
    lg = lt[0:MOE_GROUPS]
    gmax = jnp.max(lg, axis=0, keepdims=True)
    p_g = 1.0 / jnp.sum(jnp.exp(lg - gmax), axis=0, keepdims=True)
    gi = lax.broadcasted_iota(jnp.int32, lg.shape, 0).astype(F32)
    g_idx = jnp.min(jnp.where(lg == gmax, gi, float(MOE_GROUPS)), axis=0, keepdims=True)
    le = lt[8:8 + MOE_EPG]
    for g in range(1, MOE_GROUPS):
        le = jnp.where(g_idx == float(g), lt[8 + g * MOE_EPG:8 + (g + 1) * MOE_EPG], le)
    ei = lax.broadcasted_iota(jnp.int32, le.shape, 0).astype(F32)
    m1 = jnp.max(le, axis=0, keepdims=True)
    i1 = jnp.min(jnp.where(le == m1, ei, float(MOE_EPG)), axis=0, keepdims=True)
    le2 = jnp.where(ei == i1, -jnp.inf, le)
    m2 = jnp.max(le2, axis=0, keepdims=True)
    i2 = jnp.min(jnp.where(le2 == m2, ei, float(MOE_EPG)), axis=0, keepdims=True)
    r = jnp.exp(m2 - m1)
    w_first = p_g / (1.0 + r)
    eid_ref[0:1, :] = (g_idx * MOE_EPG + i1).astype(jnp.int32)
    eid_ref[1:2, :] = (g_idx * MOE_EPG + i2).astype(jnp.int32)
    rw_ref[0:1, :] = w_first
    rw_ref[1:2, :] = w_first * r


def _band_matrices(t_rows):
    t = jnp.arange(t_rows)[:, None] + POOL_HALO
    jx = jnp.arange(t_rows + POOL_HALO)[None, :]
    return jnp.stack([((jx <= t) & (jx > t - w)) for w in POOL_WINDOWS]).astype(BF16)


def _chunk_tril(t_rows):
    r = jnp.arange(t_rows)[:, None]
    c = jnp.arange(t_rows)[None, :]
    return ((r >= c) & (r // CHUNK == c // CHUNK)).astype(BF16)


def _mixer(x, ada, g_pre, g_post, g_ffn, w1, b_gate, lb_logits, hg_norm_g, pool_w, pool_scale,
           w_bh, w_bp, w_out, w_router_t, b_router):
    bsz, seq, d = x.shape
    t_rows = MIX_ROWS
    n_t = seq // t_rows
    n_tok = bsz * seq
    hgw = hg_norm_g.shape[1]
    pw = pool_scale.shape[1]
    n_r = b_router.shape[0]

    def const(shape):
        return pl.BlockSpec(shape, lambda b, j: (0,) * len(shape), pipeline_mode=pl.Buffered(1))

    in_specs = [
        pl.BlockSpec((None, t_rows, d), lambda b, j: (b, j, 0)),
        pl.BlockSpec((None, 6, d), lambda b, j: (b, 0, 0)),
        const((1, d)), const((1, d)), const((1, d)),
        const(w1.shape), const((1, 2 * d)),
        const(lb_logits.shape), const((1, hgw)),
        const((t_rows, t_rows)), const((POOL_GROUPS, t_rows, t_rows + POOL_HALO)),
        const(pool_w.shape), const((1, pw)),
        const(w_bh.shape), const(w_bp.shape), const(w_out.shape),
        const(w_router_t.shape), const((n_r, 1)),
    ]
    out_specs = [
        pl.BlockSpec((None, t_rows, d), lambda b, j: (b, j, 0)),
        pl.BlockSpec((None, t_rows, d // 2), lambda b, j: (b, j, 0)),
        pl.BlockSpec((2, t_rows), lambda b, j: (0, b * n_t + j)),
        pl.BlockSpec((2, t_rows), lambda b, j: (0, b * n_t + j)),
    ]
    out_shape = [
        jax.ShapeDtypeStruct((bsz, seq, d), F32),
        jax.ShapeDtypeStruct((bsz, seq, d // 2), jnp.uint32),
        jax.ShapeDtypeStruct((2, n_tok), jnp.int32),
        jax.ShapeDtypeStruct((2, n_tok), F32),
    ]
    scratch = [
        pltpu.VMEM((HG_HEADS, hgw // HG_HEADS, hgw // HG_HEADS), F32),
        pltpu.VMEM((t_rows + POOL_HALO, pw), BF16),
        pltpu.VMEM((t_rows, hgw), BF16),
    ]
    return pl.pallas_call(
        _mix_kernel,
        grid=(bsz, n_t),
        in_specs=in_specs, out_specs=out_specs, out_shape=out_shape, scratch_shapes=scratch,
        compiler_params=pltpu.CompilerParams(dimension_semantics=("arbitrary", "arbitrary"),
                                             vmem_limit_bytes=VMEM_LIMIT),
        name="mixer",
    )(x, ada, g_pre, g_post, g_ffn, w1, b_gate, lb_logits, hg_norm_g, _chunk_tril(t_rows),
      _band_matrices(t_rows), pool_w, pool_scale, w_bh, w_bp, w_out, w_router_t, b_router)


def _route_kernel(eid_ref, dest_ref, blk_ref, meta_ref, cnt_ref, run_ref):
    phase = pl.program_id(0)
    i = pl.program_id(1)
    cols = eid_ref.shape[1]
    eidx = lax.broadcasted_iota(jnp.int32, (N_EXPERTS, cols), 0)
    hot0 = eidx == eid_ref[0:1, :]
    hot1 = eidx == eid_ref[1:2, :]
    both = jnp.where(hot0 | hot1, 1.0, 0.0)
    tile_cnt = jnp.sum(both, axis=1, keepdims=True)

    @pl.when((phase == 0) & (i == 0))
    def _():
        cnt_ref[...] = jnp.zeros_like(cnt_ref)

    @pl.when(phase == 0)
    def _():
        cnt_ref[...] += tile_cnt

    @pl.when((phase == 1) & (i == 0))
    def _():
        nblk = jnp.floor((cnt_ref[...] + float(MOE_BLOCK - 1)) * (1.0 / MOE_BLOCK))
        nblk_f = jnp.broadcast_to(nblk, (N_EXPERTS, 128))
        er = lax.broadcasted_iota(jnp.int32, (N_EXPERTS, N_EXPERTS), 0)
        ec = lax.broadcasted_iota(jnp.int32, (N_EXPERTS, N_EXPERTS), 1)
        lower = jnp.where(ec < er, 1.0, 0.0).astype(BF16)
        start_blk = jnp.dot(lower, nblk_f.astype(BF16), preferred_element_type=F32)
        run_ref[...] = start_blk[:, 0:1] * float(MOE_BLOCK)
        end_blk = start_blk + nblk_f
        lane = lax.broadcasted_iota(jnp.int32, (N_EXPERTS, blk_ref.shape[1]), 1).astype(F32)
        done = jnp.where(end_blk[:, 0:1] <= lane, 1.0, 0.0)
        blk_ref[...] = jnp.minimum(jnp.sum(done, axis=0, keepdims=True),
                                   float(N_EXPERTS - 1)).astype(jnp.int32)
        meta_ref[...] = jnp.broadcast_to(end_blk[N_EXPERTS - 1:N_EXPERTS, 0:1],
                                         meta_ref.shape).astype(jnp.int32)

    @pl.when(phase == 1)
    def _():
        r = lax.broadcasted_iota(jnp.int32, (cols, cols), 0)
        c = lax.broadcasted_iota(jnp.int32, (cols, cols), 1)
        before = jnp.where(r < c, 1.0, 0.0).astype(BF16)
        prefix = jnp.dot(both.astype(BF16), before, preferred_element_type=F32)
        slot = run_ref[...] + prefix
        dest_ref[0:1, :] = jnp.sum(jnp.where(hot0, slot, 0.0), axis=0, keepdims=True).astype(jnp.int32)
        dest_ref[1:2, :] = jnp.sum(jnp.where(hot1, slot, 0.0), axis=0, keepdims=True).astype(jnp.int32)
        run_ref[...] += tile_cnt


def _route(eid, n_blocks):
    n_tok = eid.shape[1]
    cols = ROUTE_COLS
    blk_lanes = pl.cdiv(n_blocks, 128) * 128
    return pl.pallas_call(
        _route_kernel,
        grid=(2, n_tok // cols),
        in_specs=[pl.BlockSpec((2, cols), lambda p, i: (0, i))],
        out_specs=[pl.BlockSpec((2, cols), lambda p, i: (0, i * p)),
                   pl.BlockSpec((1, blk_lanes), lambda p, i: (0, 0)),
                   pl.BlockSpec((1, 128), lambda p, i: (0, 0))],
        out_shape=[jax.ShapeDtypeStruct((2, n_tok), jnp.int32),
                   jax.ShapeDtypeStruct((1, blk_lanes), jnp.int32),
                   jax.ShapeDtypeStruct((1, 128), jnp.int32)],
        scratch_shapes=[pltpu.VMEM((N_EXPERTS, 1), F32), pltpu.VMEM((N_EXPERTS, 1), F32)],
        compiler_params=pltpu.CompilerParams(dimension_semantics=("arbitrary", "arbitrary")),
        name="route",
    )(eid)


def _row_copy(src_ref, src_row, dst_ref, dst_row, sem):
    return pltpu.make_async_copy(src_ref.at[pl.ds(src_row, 1)], dst_ref.at[pl.ds(dst_row, 1)], sem)


def _drain_rows(rows, wait_pair):
    def drain(g, carry):
        for _ in range(MOVE_UNROLL):
            wait_pair()
        return carry

    lax.fori_loop(0, rows // MOVE_UNROLL, drain, 0)


def _dispatch_kernel(dest_ref, hn2_ref, xs_in_ref, xs_ref, sem):
    del xs_in_ref
    rows = hn2_ref.shape[0]

    def issue(g, carry):
        base = pl.multiple_of(g * MOVE_UNROLL, MOVE_UNROLL)
        group = hn2_ref.at[pl.ds(base, MOVE_UNROLL)]
        for u in range(MOVE_UNROLL):
            _row_copy(group, u, xs_ref, dest_ref[base + u], sem).start(priority=0)
            _row_copy(group, u, xs_ref, dest_ref[rows + base + u], sem).start(priority=1)
        return carry

    lax.fori_loop(0, rows // MOVE_UNROLL, issue, 0)

    def wait_pair():
        _row_copy(hn2_ref, 0, xs_ref, 0, sem).wait()
        _row_copy(hn2_ref, 0, xs_ref, 0, sem).wait()

    _drain_rows(rows, wait_pair)


def _dispatch(dest_tiles, hn2, n_rows_out):
    n_tok, d = hn2.shape
    rows = MOVE_ROWS
    xs0 = jnp.zeros((n_rows_out, d), hn2.dtype)
    return pl.pallas_call(
        _dispatch_kernel,
        grid=(n_tok // rows,),
        in_specs=[pl.BlockSpec((2 * rows,), lambda i: (i,), memory_space=pltpu.SMEM),
                  pl.BlockSpec((rows, d), lambda i: (i, 0)),
                  pl.BlockSpec(memory_space=pl.ANY)],
        out_specs=pl.BlockSpec(memory_space=pl.ANY),
        out_shape=jax.ShapeDtypeStruct((n_rows_out, d), hn2.dtype),
        scratch_shapes=[pltpu.SemaphoreType.DMA],
        input_output_aliases={2: 0},
        compiler_params=pltpu.CompilerParams(dimension_semantics=("arbitrary",)),
        name="dispatch",
    )(dest_tiles, hn2, xs0)


def _expert_kernel(blk_ref, meta_ref, xs_ref, wg_ref, wu_ref, wd_ref, ys_ref, wg_s, wu_s, wd_s):
    j = pl.program_id(0)
    used = j < meta_ref[0]
    first_of_expert = (j == 0) | (blk_ref[j] != blk_ref[jnp.maximum(j - 1, 0)])

    @pl.when(used & first_of_expert)
    def _():
        wg_s[...] = wg_ref[...].astype(BF16)
        wu_s[...] = wu_ref[...].astype(BF16)
        wd_s[...] = wd_ref[...].astype(BF16)

    @pl.when(used)
    def _():
        xb = _unpack_bf16_pairs(xs_ref[...]).astype(BF16)
        gp = jnp.dot(xb, wg_s[...], preferred_element_type=F32)
        up = jnp.dot(xb, wu_s[...], preferred_element_type=F32)
        act = gp * _sigmoid(gp) * up
        ys_ref[...] = _pack_bf16_pairs(jnp.dot(act.astype(BF16), wd_s[...], preferred_element_type=F32))

    @pl.when(jnp.logical_not(used))
    def _():
        ys_ref[...] = jnp.zeros_like(ys_ref)


def _experts(block_e, n_used, xs, w_gate, w_up, w_down, n_blocks):
    dp = xs.shape[1]
    ff, d = w_down.shape[1], w_down.shape[2]

    def row_block(j, be, nu):
        return (jnp.minimum(j, nu[0] - 1), 0)

    def expert_block(j, be, nu):
        return (be[jnp.minimum(j, nu[0] - 1)], 0, 0)

    grid_spec = pltpu.PrefetchScalarGridSpec(
        num_scalar_prefetch=2,
        grid=(n_blocks,),
        in_specs=[pl.BlockSpec((MOE_BLOCK, dp), row_block),
                  pl.BlockSpec((None, d, ff), expert_block),
                  pl.BlockSpec((None, d, ff), expert_block),
                  pl.BlockSpec((None, ff, d), expert_block)],
        out_specs=pl.BlockSpec((MOE_BLOCK, dp), lambda j, be, nu: (j, 0)),
        scratch_shapes=[pltpu.VMEM((d, ff), BF16), pltpu.VMEM((d, ff), BF16), pltpu.VMEM((ff, d), BF16)],
    )
    return pl.pallas_call(
        _expert_kernel,
        grid_spec=grid_spec,
        out_shape=jax.ShapeDtypeStruct(xs.shape, xs.dtype),
        compiler_params=pltpu.CompilerParams(dimension_semantics=("arbitrary",),
                                             vmem_limit_bytes=VMEM_LIMIT),
        name="experts",
    )(block_e, n_used, xs, w_gate, w_up, w_down)


def _combine_kernel(dest_ref, ys_ref, rw_ref, h1_ref, ada_ref, g_ref, out_ref, buf0, buf1, sem):
    rows = h1_ref.shape[0]

    def issue(g, carry):
        base = pl.multiple_of(g * MOVE_UNROLL, MOVE_UNROLL)
        group0 = buf0.at[pl.ds(base, MOVE_UNROLL)]
        group1 = buf1.at[pl.ds(base, MOVE_UNROLL)]
        for u in range(MOVE_UNROLL):
            _row_copy(ys_ref, dest_ref[base + u], group0, u, sem).start(priority=0)
            _row_copy(ys_ref, dest_ref[rows + base + u], group1, u, sem).start(priority=1)
        return carry

    lax.fori_loop(0, rows // MOVE_UNROLL, issue, 0)

    def wait_pair():
        _row_copy(ys_ref, 0, buf0, 0, sem).wait()
        _row_copy(ys_ref, 0, buf1, 0, sem).wait()

    _drain_rows(rows, wait_pair)

    rw = rw_ref[...]
    y = rw[:, 0:1] * _unpack_bf16_pairs(buf0[...]) + rw[:, 1:2] * _unpack_bf16_pairs(buf1[...])
    gt2 = ada_ref[5:6, :]
    out_ref[...] = h1_ref[...] + gt2 * _rms_norm(y, g_ref[...])


def _combine(dest_tiles, ys, rw_cols, h1, ada, g_post, seq):
    n_tok, d = h1.shape
    rows = MOVE_ROWS
    per_batch = seq // rows
    return pl.pallas_call(
        _combine_kernel,
        grid=(n_tok // rows,),
        in_specs=[pl.BlockSpec((2 * rows,), lambda i: (i,), memory_space=pltpu.SMEM),
                  pl.BlockSpec(memory_space=pl.ANY),
                  pl.BlockSpec((rows, 2), lambda i: (i, 0)),
                  pl.BlockSpec((rows, d), lambda i: (i, 0)),
                  pl.BlockSpec((None, 6, d), lambda i: (i // per_batch, 0, 0)),
                  pl.BlockSpec((1, d), lambda i: (0, 0))],
        out_specs=pl.BlockSpec((rows, d), lambda i: (i, 0)),
        out_shape=jax.ShapeDtypeStruct((n_tok, d), F32),
        scratch_shapes=[pltpu.VMEM((rows, ys.shape[1]), ys.dtype), pltpu.VMEM((rows, ys.shape[1]), ys.dtype),
                        pltpu.SemaphoreType.DMA],
        compiler_params=pltpu.CompilerParams(dimension_semantics=("arbitrary",)),
        name="combine",
    )(dest_tiles, ys, rw_cols, h1, ada, g_post)


def kernel(x, c, w_ada, b_ada, g_pre_mix, g_post_mix, w_in, hg_lb_logits, hg_norm_g, pool_w, pool_scale,
           w_branch_hg, w_branch_pool, w_gate, b_gate, w_out, g_pre_ffn, g_post_ffn, w_router_group,
           b_router_group, w_router_expert, b_router_expert, w_exp_gate, w_exp_up, w_exp_down):
    depth = w_in.shape[0]
    bsz, seq, d = x.shape
    n_tok = bsz * seq
    n_blocks = -(-(n_tok * 2) // MOE_BLOCK) + N_EXPERTS
    assert seq % MIX_ROWS == 0 and MIX_ROWS % CHUNK == 0 and seq % MOVE_ROWS == 0
    assert n_tok % ROUTE_COLS == 0 and hg_lb_logits.shape[0] == 2 and depth == 1

    h = x
    for l in range(depth):
        ada = _ada(c, w_ada[l], b_ada[l]).reshape(bsz, 6, d)
        w1 = jnp.concatenate([w_in[l], w_gate[l]], axis=1).astype(BF16)
        pad_g = jnp.zeros((d, 8 - MOE_GROUPS), F32)
        pad_e = jnp.zeros((d, ROUTER_COLS - 8 - N_EXPERTS), F32)
        w_router = jnp.concatenate([w_router_group[l], pad_g, w_router_expert[l], pad_e], axis=1)
        w_router_hi = w_router.astype(BF16)
        w_router_lo = (w_router - w_router_hi.astype(F32)).astype(BF16)
        w_router_t = jnp.concatenate([w_router_hi, w_router_lo], axis=1)
        b_router = jnp.concatenate(
            [b_router_group[l], jnp.zeros((8 - MOE_GROUPS,), F32), b_router_expert[l],
             jnp.zeros((ROUTER_COLS - 8 - N_EXPERTS,), F32)])[:, None]
        h1, hn2, eid, rw = _mixer(
            h, ada, g_pre_mix[l][None], g_post_mix[l][None], g_pre_ffn[l][None], w1, b_gate[l][None],
            hg_lb_logits, hg_norm_g[l][None], pool_w[l].astype(BF16), pool_scale[l][None],
            w_branch_hg[l].astype(BF16), w_branch_pool[l].astype(BF16), w_out[l].astype(BF16),
            w_router_t, b_router)

        dest, block_e, meta = _route(eid, n_blocks)
        n_mt = n_tok // MOVE_ROWS
        dest_tiles = dest.reshape(2, n_mt, MOVE_ROWS).transpose(1, 0, 2).reshape(-1)
        xs = _dispatch(dest_tiles, hn2.reshape(n_tok, d // 2), n_blocks * MOE_BLOCK)
        ys = _experts(block_e[0, :n_blocks], meta[0, :1], xs, w_exp_gate[l], w_exp_up[l], w_exp_down[l],
                      n_blocks)
        h = _combine(dest_tiles, ys, rw.T, h1.reshape(n_tok, d), ada, g_post_ffn[l][None], seq)
        h = h.reshape(bsz, seq, d)
    return h
```

```python
import jax
import jax.numpy as jnp
from jax import lax
from jax.experimental import pallas as pl
from jax.experimental.pallas import tpu as pltpu

F32 = jnp.float32
BF16 = jnp.bfloat16
HIGHEST = lax.Precision.HIGHEST

CHUNK = 64
HG_HEADS = 4
POOL_WINDOWS = (2, 4, 8, 16)
POOL_GROUPS = 4
MOE_GROUPS = 4
MOE_EPG = 8
N_EXPERTS = MOE_GROUPS * MOE_EPG
MOE_BLOCK = 256
EPS = 1e-6

MIX_ROWS = 256
MIX_SUBS = 2
POOL_HALO = 128
ROUTE_COLS = 512
MOVE_ROWS = 512
MOVE_UNROLL = 8
ROUTER_COLS = 128
LOG_DECAY_FLOOR = -80.0
VMEM_LIMIT = 56 * 1024 * 1024

NT_DIMS = (((1,), (1,)), ((), ()))
TN_DIMS = (((0,), (0,)), ((), ()))


def _sigmoid(v):
    return 0.5 * jnp.tanh(0.5 * v) + 0.5


def _rms_norm(v, g):
    return v * lax.rsqrt(jnp.mean(v * v, axis=-1, keepdims=True) + EPS) * g


def _pack_bf16_pairs(v):
    n = v.shape[1] // 2
    lo = lax.bitcast_convert_type(v[:, :n].astype(BF16).astype(F32), jnp.uint32)
    hi = lax.bitcast_convert_type(v[:, n:].astype(BF16).astype(F32), jnp.uint32)
    return hi | (lo >> 16)


def _unpack_bf16_pairs(p):
    lo = lax.bitcast_convert_type(p << 16, F32)
    hi = lax.bitcast_convert_type(p & jnp.uint32(0xFFFF0000), F32)
    return jnp.concatenate([lo, hi], axis=1)


def _ada_kernel(c_ref, w_ref, b_ref, o_ref):
    c = c_ref[...]
    cond = c * _sigmoid(c)
    o_ref[...] = jnp.dot(cond, w_ref[...], preferred_element_type=F32, precision=HIGHEST) + b_ref[...]


def _ada(c, w_ada, b_ada):
    bsz, d = c.shape
    n_out = w_ada.shape[1]
    return pl.pallas_call(
        _ada_kernel,
        grid=(n_out // d,),
        in_specs=[pl.BlockSpec((bsz, d), lambda i: (0, 0)),
                  pl.BlockSpec((d, d), lambda i: (0, i)),
                  pl.BlockSpec((1, d), lambda i: (0, i))],
        out_specs=pl.BlockSpec((bsz, d), lambda i: (0, i)),
        out_shape=jax.ShapeDtypeStruct((bsz, n_out), F32),
        name="ada",
    )(c, w_ada, b_ada.reshape(1, n_out))


class _Sub:
    pass


def _mix_kernel(x_ref, ada_ref, gpre_ref, gpost_ref, gffn_ref, w1_ref, bgate_ref, lbl_ref, hgn_ref,
                tril_ref, band_ref, poolw_ref, pools_ref, wbh_ref, wbp_ref, wout_ref, wr_ref, br_ref,
                h1_ref, hn2_ref, eid_ref, rw_ref,
                st_ref, ext_ref, o_scr):
    step_rows, d = x_ref.shape
    t_rows = MIX_ROWS
    hgw = hgn_ref.shape[1]
    hd = hgw // HG_HEADS
    pw = pools_ref.shape[1]
    gd = pw // POOL_GROUPS
    n_chunks = t_rows // CHUNK
    c_u = 4 * hgw
    c_gate = c_u + pw
    j = pl.program_id(1)

    @pl.when(j == 0)
    def _():
        st_ref[...] = jnp.zeros_like(st_ref)
        ext_ref[step_rows:step_rows + POOL_HALO, :] = jnp.zeros((POOL_HALO, pw), BF16)

    ext_ref[0:POOL_HALO, :] = ext_ref[step_rows:step_rows + POOL_HALO, :]

    ada = ada_ref[...]
    sh1, sc1, gt1 = ada[0:1], ada[1:2], ada[2:3]
    sh2, sc2 = ada[3:4], ada[4:5]
    lbl = lbl_ref[...]
    lmax = jnp.maximum(lbl[0:1], lbl[1:2])
    e0 = jnp.exp(lbl[0:1] - lmax)
    lb = e0 / (e0 + jnp.exp(lbl[1:2] - lmax))
    row = lax.broadcasted_iota(jnp.int32, (CHUNK, CHUNK), 0)
    col = lax.broadcasted_iota(jnp.int32, (CHUNK, CHUNK), 1)
    causal = row >= col

    def chunk_heads():
        for ci in range(n_chunks):
            for h in range(HG_HEADS):
                yield ci, h, slice(ci * CHUNK, (ci + 1) * CHUNK), slice(h * hd, (h + 1) * hd)

    def m_project(c):
        x = x_ref[c.rows, :]
        hn = _rms_norm(x, gpre_ref[...]) * (1.0 + sc1) + sh1
        c.proj = jnp.dot(hn.astype(BF16), w1_ref[...], preferred_element_type=F32)

    def e_gates(c):
        qr, fr = c.proj[:, 0:hgw], c.proj[:, hgw:2 * hgw]
        vr, gr = c.proj[:, 2 * hgw:3 * hgw], c.proj[:, 3 * hgw:4 * hgw]
        c.q = qr * _sigmoid(qr)
        f = lb + (1.0 - lb) * _sigmoid(fr)
        c.k = 1.0 - f
        lf = jnp.log(f)
        c.lf_hi = lf.astype(BF16)
        c.lf_lo = (lf - c.lf_hi.astype(F32)).astype(BF16)
        c.vb = vr.astype(BF16)
        c.og = hgn_ref[...] * (gr * _sigmoid(gr))
        c.u = c.proj[:, c_u:c_gate]
        ext_ref[c.ext0 + POOL_HALO:c.ext0 + POOL_HALO + t_rows, :] = c.u.astype(BF16)

    def m_cumsum(c):
        tril = tril_ref[...]
        c.b = (jnp.dot(tril, c.lf_hi, preferred_element_type=F32)
               + jnp.dot(tril, c.lf_lo, preferred_element_type=F32))

    def e_decays(c):
        bc = jnp.maximum(c.b, LOG_DECAY_FLOOR)
        c.qt = (c.q * jnp.exp(bc)).astype(BF16)
        c.kt = (c.k * jnp.exp(-bc)).astype(BF16)
        c.kh, c.dec = [], []
        for ci in range(n_chunks):
            b_c = c.b[ci * CHUNK:(ci + 1) * CHUNK, :]
            b_last = b_c[CHUNK - 1:CHUNK, :]
            c.kh.append((c.k[ci * CHUNK:(ci + 1) * CHUNK, :] * jnp.exp(b_last - b_c)).astype(BF16))
            c.dec.append(jnp.exp(b_last))

    def m_scores(c):
        c.s, c.upd = {}, {}
        for ci, h, rs, cs in chunk_heads():
            c.s[ci, h] = lax.dot_general(c.qt[rs, cs], c.kt[rs, cs], NT_DIMS, preferred_element_type=F32)
            c.upd[ci, h] = lax.dot_general(c.vb[rs, cs], c.kh[ci][:, cs], TN_DIMS,
                                           preferred_element_type=F32)

    def e_states(c):
        c.sb, c.stb = {}, {}
        for ci, h, rs, cs in chunk_heads():
            c.sb[ci, h] = jnp.where(causal, c.s[ci, h], 0.0).astype(BF16)
        for h in range(HG_HEADS):
            st = st_ref[h]
            for ci in range(n_chunks):
                c.stb[ci, h] = st.astype(BF16)
                st = c.dec[ci][:, h * hd:(h + 1) * hd] * st + c.upd[ci, h]
            st_ref[h] = st

    def m_outputs(c):
        c.o = {}
        for ci, h, rs, cs in chunk_heads():
            c.o[ci, h] = (jnp.dot(c.sb[ci, h], c.vb[rs, cs], preferred_element_type=F32)
                          + lax.dot_general(c.qt[rs, cs], c.stb[ci, h], NT_DIMS, preferred_element_type=F32))

    def e_head_norm(c):
        for ci, h, rs, cs in chunk_heads():
            o = c.o[ci, h]
            o = o * lax.rsqrt(jnp.mean(o * o, axis=-1, keepdims=True) + EPS)
            o_scr[c.row0 + ci * CHUNK:c.row0 + (ci + 1) * CHUNK, cs] = (o * c.og[rs, cs]).astype(BF16)

    def m_branch_hg(c):
        c.y_hg = jnp.dot(o_scr[c.rows, :], wbh_ref[...], preferred_element_type=F32)
        ext = ext_ref[c.ext0:c.ext0 + POOL_HALO + t_rows, :]
        c.wsum = [jnp.dot(band_ref[g], ext[:, g * gd:(g + 1) * gd], preferred_element_type=F32)
                  for g in range(POOL_GROUPS)]

    def e_pooled(c):
        pos = j * step_rows + c.row0 + lax.broadcasted_iota(jnp.int32, (t_rows, gd), 0)
        c.pooled = []
        for g in range(POOL_GROUPS):
            cnt = jnp.minimum(pos + 1, POOL_WINDOWS[g]).astype(F32)
            c.pooled.append((c.wsum[g] / cnt - c.u[:, g * gd:(g + 1) * gd]).astype(BF16))

    def m_pool_mix(c):
        c.mixed = [jnp.dot(c.pooled[g], poolw_ref[g], preferred_element_type=F32) for g in range(POOL_GROUPS)]

    def e_gate(c):
        c.mixed = (jnp.concatenate(c.mixed, axis=1) * pools_ref[...]).astype(BF16)
        c.gate = _sigmoid(c.proj[:, c_gate:] + bgate_ref[...])

    def m_branch_pool(c):
        c.y_pool = jnp.dot(c.mixed, wbp_ref[...], preferred_element_type=F32)

    def e_merge(c):
        c.merged = (c.gate[:, 0:d] * c.y_hg + c.gate[:, d:2 * d] * c.y_pool).astype(BF16)

    def m_out(c):
        c.mix = jnp.dot(c.merged, wout_ref[...], preferred_element_type=F32)

    def e_residual(c):
        h1 = x_ref[c.rows, :] + gt1 * _rms_norm(c.mix, gpost_ref[...])
        h1_ref[c.rows, :] = h1
        hn2 = _rms_norm(h1, gffn_ref[...]) * (1.0 + sc2) + sh2
        hn2_ref[c.rows, :] = _pack_bf16_pairs(hn2)
        hn2_hi = hn2.astype(BF16)
        c.hn2_split = jnp.concatenate([hn2_hi, (hn2 - hn2_hi.astype(F32)).astype(BF16)], axis=0)

    def m_router(c):
        c.prod = jnp.dot(c.hn2_split, wr_ref[...], preferred_element_type=F32)

    def e_route(c):
        n_r = wr_ref.shape[1] // 2
        logits = (c.prod[0:t_rows, 0:n_r] + c.prod[t_rows:2 * t_rows, 0:n_r]
                  + c.prod[0:t_rows, n_r:2 * n_r])
        lt = logits.T + br_ref[...]
        lg = lt[0:MOE_GROUPS]
        gmax = jnp.max(lg, axis=0, keepdims=True)
        p_g = 1.0 / jnp.sum(jnp.exp(lg - gmax), axis=0, keepdims=True)
        gi = lax.broadcasted_iota(jnp.int32, lg.shape, 0).astype(F32)
        g_idx = jnp.min(jnp.where(lg == gmax, gi, float(MOE_GROUPS)), axis=0, keepdims=True)
        le = lt[8:8 + MOE_EPG]
        for g in range(1, MOE_GROUPS):
            le = jnp.where(g_idx == float(g), lt[8 + g * MOE_EPG:8 + (g + 1) * MOE_EPG], le)
        ei = lax.broadcasted_iota(jnp.int32, le.shape, 0).astype(F32)
        m1 = jnp.max(le, axis=0, keepdims=True)
        i1 = jnp.min(jnp.where(le == m1, ei, float(MOE_EPG)), axis=0, keepdims=True)
        le2 = jnp.where(ei == i1, -jnp.inf, le)
        m2 = jnp.max(le2, axis=0, keepdims=True)
        i2 = jnp.min(jnp.where(le2 == m2, ei, float(MOE_EPG)), axis=0, keepdims=True)
        r = jnp.exp(m2 - m1)
        w_first = p_g / (1.0 + r)
        eid_ref[0:1, c.rows] = (g_idx * MOE_EPG + i1).astype(jnp.int32)
        eid_ref[1:2, c.rows] = (g_idx * MOE_EPG + i2).astype(jnp.int32)
        rw_ref[0:1, c.rows] = w_first
        rw_ref[1:2, c.rows] = w_first * r

    stages = [m_project, e_gates, m_cumsum, e_decays, m_scores, e_states, m_outputs, e_head_norm,
              m_branch_hg, e_pooled, m_pool_mix, e_gate, m_branch_pool, e_merge, m_out, e_residual,
              m_router, e_route]

    subs = []
    for k in range(step_rows // t_rows):
        c = _Sub()
        c.row0 = k * t_rows
        c.rows = slice(c.row0, c.row0 + t_rows)
        c.ext0 = k * t_rows
        subs.append(c)
    for t in range(len(stages) + len(subs) - 1):
        for k, c in enumerate(subs):
            if 0 <= t - k < len(stages):
                stages[t - k](c)


def _band_matrices(t_rows):
    t = jnp.arange(t_rows)[:, None] + POOL_HALO
    jx = jnp.arange(t_rows + POOL_HALO)[None, :]
    return jnp.stack([((jx <= t) & (jx > t - w)) for w in POOL_WINDOWS]).astype(BF16)


def _chunk_tril(t_rows):
    r = jnp.arange(t_rows)[:, None]
    c = jnp.arange(t_rows)[None, :]
    return ((r >= c) & (r // CHUNK == c // CHUNK)).astype(BF16)


def _mixer(x, ada, g_pre, g_post, g_ffn, w1, b_gate, lb_logits, hg_norm_g, pool_w, pool_scale,
           w_bh, w_bp, w_out, w_router_t, b_router):
    bsz, seq, d = x.shape
    t_rows = MIX_ROWS
    step_rows = MIX_ROWS * MIX_SUBS
    n_t = seq // step_rows
    n_tok = bsz * seq
    hgw = hg_norm_g.shape[1]
    pw = pool_scale.shape[1]
    n_r = b_router.shape[0]

    def const(shape):
        return pl.BlockSpec(shape, lambda b, j: (0,) * len(shape), pipeline_mode=pl.Buffered(1))

    in_specs = [
        pl.BlockSpec((None, step_rows, d), lambda b, j: (b, j, 0)),
        pl.BlockSpec((None, 6, d), lambda b, j: (b, 0, 0)),
        const((1, d)), const((1, d)), const((1, d)),
        const(w1.shape), const((1, 2 * d)),
        const(lb_logits.shape), const((1, hgw)),
        const((t_rows, t_rows)), const((POOL_GROUPS, t_rows, t_rows + POOL_HALO)),
        const(pool_w.shape), const((1, pw)),
        const(w_bh.shape), const(w_bp.shape), const(w_out.shape),
        const(w_router_t.shape), const((n_r, 1)),
    ]
    out_specs = [
        pl.BlockSpec((None, step_rows, d), lambda b, j: (b, j, 0)),
        pl.BlockSpec((None, step_rows, d // 2), lambda b, j: (b, j, 0)),
        pl.BlockSpec((2, step_rows), lambda b, j: (0, b * n_t + j)),
        pl.BlockSpec((2, step_rows), lambda b, j: (0, b * n_t + j)),
    ]
    out_shape = [
        jax.ShapeDtypeStruct((bsz, seq, d), F32),
        jax.ShapeDtypeStruct((bsz, seq, d // 2), jnp.uint32),
        jax.ShapeDtypeStruct((2, n_tok), jnp.int32),
        jax.ShapeDtypeStruct((2, n_tok), F32),
    ]
    scratch = [
        pltpu.VMEM((HG_HEADS, hgw // HG_HEADS, hgw // HG_HEADS), F32),
        pltpu.VMEM((step_rows + POOL_HALO, pw), BF16),
        pltpu.VMEM((step_rows, hgw), BF16),
    ]
    return pl.pallas_call(
        _mix_kernel,
        grid=(bsz, n_t),
        in_specs=in_specs, out_specs=out_specs, out_shape=out_shape, scratch_shapes=scratch,
        compiler_params=pltpu.CompilerParams(dimension_semantics=("arbitrary", "arbitrary"),
                                             vmem_limit_bytes=VMEM_LIMIT),
        name="mixer",
    )(x, ada, g_pre, g_post, g_ffn, w1, b_gate, lb_logits, hg_norm_g, _chunk_tril(t_rows),
      _band_matrices(t_rows), pool_w, pool_scale, w_bh, w_bp, w_out, w_router_t, b_router)


def _route_kernel(eid_ref, dest_ref, blk_ref, meta_ref, cnt_ref, run_ref):
    phase = pl.program_id(0)
    i = pl.program_id(1)
    cols = eid_ref.shape[1]
    eidx = lax.broadcasted_iota(jnp.int32, (N_EXPERTS, cols), 0)
    hot0 = eidx == eid_ref[0:1, :]
    hot1 = eidx == eid_ref[1:2, :]
    both = jnp.where(hot0 | hot1, 1.0, 0.0)
    tile_cnt = jnp.sum(both, axis=1, keepdims=True)

    @pl.when((phase == 0) & (i == 0))
    def _():
        cnt_ref[...] = jnp.zeros_like(cnt_ref)

    @pl.when(phase == 0)
    def _():
        cnt_ref[...] += tile_cnt

    @pl.when((phase == 1) & (i == 0))
    def _():
        nblk = jnp.floor((cnt_ref[...] + float(MOE_BLOCK - 1)) * (1.0 / MOE_BLOCK))
        nblk_f = jnp.broadcast_to(nblk, (N_EXPERTS, 128))
        er = lax.broadcasted_iota(jnp.int32, (N_EXPERTS, N_EXPERTS), 0)
        ec = lax.broadcasted_iota(jnp.int32, (N_EXPERTS, N_EXPERTS), 1)
        lower = jnp.where(ec < er, 1.0, 0.0).astype(BF16)
        start_blk = jnp.dot(lower, nblk_f.astype(BF16), preferred_element_type=F32)
        run_ref[...] = start_blk[:, 0:1] * float(MOE_BLOCK)
        end_blk = start_blk + nblk_f
        lane = lax.broadcasted_iota(jnp.int32, (N_EXPERTS, blk_ref.shape[1]), 1).astype(F32)
        done = jnp.where(end_blk[:, 0:1] <= lane, 1.0, 0.0)
        blk_ref[...] = jnp.minimum(jnp.sum(done, axis=0, keepdims=True),
                                   float(N_EXPERTS - 1)).astype(jnp.int32)
        meta_ref[...] = jnp.broadcast_to(end_blk[N_EXPERTS - 1:N_EXPERTS, 0:1],
                                         meta_ref.shape).astype(jnp.int32)

    @pl.when(phase == 1)
    def _():
        r = lax.broadcasted_iota(jnp.int32, (cols, cols), 0)
        c = lax.broadcasted_iota(jnp.int32, (cols, cols), 1)
        before = jnp.where(r < c, 1.0, 0.0).astype(BF16)
        prefix = jnp.dot(both.astype(BF16), before, preferred_element_type=F32)
        slot = run_ref[...] + prefix
        dest_ref[0:1, :] = jnp.sum(jnp.where(hot0, slot, 0.0), axis=0, keepdims=True).astype(jnp.int32)
        dest_ref[1:2, :] = jnp.sum(jnp.where(hot1, slot, 0.0), axis=0, keepdims=True).astype(jnp.int32)
        run_ref[...] += tile_cnt


def _route(eid, n_blocks):
    n_tok = eid.shape[1]
    cols = ROUTE_COLS
    blk_lanes = pl.cdiv(n_blocks, 128) * 128
    return pl.pallas_call(
        _route_kernel,
        grid=(2, n_tok // cols),
        in_specs=[pl.BlockSpec((2, cols), lambda p, i: (0, i))],
        out_specs=[pl.BlockSpec((2, cols), lambda p, i: (0, i * p)),
                   pl.BlockSpec((1, blk_lanes), lambda p, i: (0, 0)),
                   pl.BlockSpec((1, 128), lambda p, i: (0, 0))],
        out_shape=[jax.ShapeDtypeStruct((2, n_tok), jnp.int32),
                   jax.ShapeDtypeStruct((1, blk_lanes), jnp.int32),
                   jax.ShapeDtypeStruct((1, 128), jnp.int32)],
        scratch_shapes=[pltpu.VMEM((N_EXPERTS, 1), F32), pltpu.VMEM((N_EXPERTS, 1), F32)],
        compiler_params=pltpu.CompilerParams(dimension_semantics=("arbitrary", "arbitrary")),
        name="route",
    )(eid)


def _row_copy(src_ref, src_row, dst_ref, dst_row, sem):
    return pltpu.make_async_copy(src_ref.at[pl.ds(src_row, 1)], dst_ref.at[pl.ds(dst_row, 1)], sem)


def _drain_rows(rows, wait_pair):
    def drain(g, carry):
        for _ in range(MOVE_UNROLL):
            wait_pair()
        return carry

    lax.fori_loop(0, rows // MOVE_UNROLL, drain, 0)


def _dispatch_kernel(dest_ref, hn2_ref, xs_in_ref, xs_ref, sem):
    del xs_in_ref
    rows = hn2_ref.shape[0]

    def issue(g, carry):
        base = pl.multiple_of(g * MOVE_UNROLL, MOVE_UNROLL)
        group = hn2_ref.at[pl.ds(base, MOVE_UNROLL)]
        for u in range(MOVE_UNROLL):
            _row_copy(group, u, xs_ref, dest_ref[base + u], sem).start(priority=0)
            _row_copy(group, u, xs_ref, dest_ref[rows + base + u], sem).start(priority=1)
        return carry

    lax.fori_loop(0, rows // MOVE_UNROLL, issue, 0)

    def wait_pair():
        _row_copy(hn2_ref, 0, xs_ref, 0, sem).wait()
        _row_copy(hn2_ref, 0, xs_ref, 0, sem).wait()

    _drain_rows(rows, wait_pair)


def _dispatch(dest_tiles, hn2, n_rows_out):
    n_tok, d = hn2.shape
    rows = MOVE_ROWS
    xs0 = jnp.zeros((n_rows_out, d), hn2.dtype)
    return pl.pallas_call(
        _dispatch_kernel,
        grid=(n_tok // rows,),
        in_specs=[pl.BlockSpec((2 * rows,), lambda i: (i,), memory_space=pltpu.SMEM),
                  pl.BlockSpec((rows, d), lambda i: (i, 0)),
                  pl.BlockSpec(memory_space=pl.ANY)],
        out_specs=pl.BlockSpec(memory_space=pl.ANY),
        out_shape=jax.ShapeDtypeStruct((n_rows_out, d), hn2.dtype),
        scratch_shapes=[pltpu.SemaphoreType.DMA],
        input_output_aliases={2: 0},
        compiler_params=pltpu.CompilerParams(dimension_semantics=("arbitrary",)),
        name="dispatch",
    )(dest_tiles, hn2, xs0)


def _expert_kernel(blk_ref, meta_ref, xs_ref, wg_ref, wu_ref, wd_ref, ys_ref, wg_s, wu_s, wd_s):
    j = pl.program_id(0)
    used = j < meta_ref[0]
    first_of_expert = (j == 0) | (blk_ref[j] != blk_ref[jnp.maximum(j - 1, 0)])

    @pl.when(used & first_of_expert)
    def _():
        wg_s[...] = wg_ref[...].astype(BF16)
        wu_s[...] = wu_ref[...].astype(BF16)
        wd_s[...] = wd_ref[...].astype(BF16)

    @pl.when(used)
    def _():
        xb = _unpack_bf16_pairs(xs_ref[...]).astype(BF16)
        gp = jnp.dot(xb, wg_s[...], preferred_element_type=F32)
        up = jnp.dot(xb, wu_s[...], preferred_element_type=F32)
        act = gp * _sigmoid(gp) * up
        ys_ref[...] = _pack_bf16_pairs(jnp.dot(act.astype(BF16), wd_s[...], preferred_element_type=F32))

    @pl.when(jnp.logical_not(used))
    def _():
        ys_ref[...] = jnp.zeros_like(ys_ref)


def _experts(block_e, n_used, xs, w_gate, w_up, w_down, n_blocks):
    dp = xs.shape[1]
    ff, d = w_down.shape[1], w_down.shape[2]

    def row_block(j, be, nu):
        return (jnp.minimum(j, nu[0] - 1), 0)

    def expert_block(j, be, nu):
        return (be[jnp.minimum(j, nu[0] - 1)], 0, 0)

    grid_spec = pltpu.PrefetchScalarGridSpec(
        num_scalar_prefetch=2,
        grid=(n_blocks,),
        in_specs=[pl.BlockSpec((MOE_BLOCK, dp), row_block),
                  pl.BlockSpec((None, d, ff), expert_block),
                  pl.BlockSpec((None, d, ff), expert_block),
                  pl.BlockSpec((None, ff, d), expert_block)],
        out_specs=pl.BlockSpec((MOE_BLOCK, dp), lambda j, be, nu: (j, 0)),
        scratch_shapes=[pltpu.VMEM((d, ff), BF16), pltpu.VMEM((d, ff), BF16), pltpu.VMEM((ff, d), BF16)],
    )
    return pl.pallas_call(
        _expert_kernel,
        grid_spec=grid_spec,
        out_shape=jax.ShapeDtypeStruct(xs.shape, xs.dtype),
        compiler_params=pltpu.CompilerParams(dimension_semantics=("arbitrary",),
                                             vmem_limit_bytes=VMEM_LIMIT),
        name="experts",
    )(block_e, n_used, xs, w_gate, w_up, w_down)


def _combine_kernel(dest_ref, ys_ref, rw_ref, h1_ref, ada_ref, g_ref, out_ref, buf0, buf1, sem):
    rows = h1_ref.shape[0]

    def issue(g, carry):
        base = pl.multiple_of(g * MOVE_UNROLL, MOVE_UNROLL)
        group0 = buf0.at[pl.ds(base, MOVE_UNROLL)]
        group1 = buf1.at[pl.ds(base, MOVE_UNROLL)]
        for u in range(MOVE_UNROLL):
            _row_copy(ys_ref, dest_ref[base + u], group0, u, sem).start(priority=0)
            _row_copy(ys_ref, dest_ref[rows + base + u], group1, u, sem).start(priority=1)
        return carry

    lax.fori_loop(0, rows // MOVE_UNROLL, issue, 0)

    def wait_pair():
        _row_copy(ys_ref, 0, buf0, 0, sem).wait()
        _row_copy(ys_ref, 0, buf1, 0, sem).wait()

    _drain_rows(rows, wait_pair)

    rw = rw_ref[...]
    y = rw[:, 0:1] * _unpack_bf16_pairs(buf0[...]) + rw[:, 1:2] * _unpack_bf16_pairs(buf1[...])
    gt2 = ada_ref[5:6, :]
    out_ref[...] = h1_ref[...] + gt2 * _rms_norm(y, g_ref[...])


def _combine(dest_tiles, ys, rw_cols, h1, ada, g_post, seq):
    n_tok, d = h1.shape
    rows = MOVE_ROWS
    per_batch = seq // rows
    return pl.pallas_call(
        _combine_kernel,
        grid=(n_tok // rows,),
        in_specs=[pl.BlockSpec((2 * rows,), lambda i: (i,), memory_space=pltpu.SMEM),
                  pl.BlockSpec(memory_space=pl.ANY),
                  pl.BlockSpec((rows, 2), lambda i: (i, 0)),
                  pl.BlockSpec((rows, d), lambda i: (i, 0)),
                  pl.BlockSpec((None, 6, d), lambda i: (i // per_batch, 0, 0)),
                  pl.BlockSpec((1, d), lambda i: (0, 0))],
        out_specs=pl.BlockSpec((rows, d), lambda i: (i, 0)),
        out_shape=jax.ShapeDtypeStruct((n_tok, d), F32),
        scratch_shapes=[pltpu.VMEM((rows, ys.shape[1]), ys.dtype), pltpu.VMEM((rows, ys.shape[1]), ys.dtype),
                        pltpu.SemaphoreType.DMA],
        compiler_params=pltpu.CompilerParams(dimension_semantics=("arbitrary",)),
        name="combine",
    )(dest_tiles, ys, rw_cols, h1, ada, g_post)


def kernel(x, c, w_ada, b_ada, g_pre_mix, g_post_mix, w_in, hg_lb_logits, hg_norm_g, pool_w, pool_scale,
           w_branch_hg, w_branch_pool, w_gate, b_gate, w_out, g_pre_ffn, g_post_ffn, w_router_group,
           b_router_group, w_router_expert, b_router_expert, w_exp_gate, w_exp_up, w_exp_down):
    depth = w_in.shape[0]
    bsz, seq, d = x.shape
    n_tok = bsz * seq
    n_blocks = -(-(n_tok * 2) // MOE_BLOCK) + N_EXPERTS
    assert seq % (MIX_ROWS * MIX_SUBS) == 0 and MIX_ROWS % CHUNK == 0 and seq % MOVE_ROWS == 0
    assert n_tok % ROUTE_COLS == 0 and hg_lb_logits.shape[0] == 2 and depth == 1

    h = x
    for l in range(depth):
        ada = _ada(c, w_ada[l], b_ada[l]).reshape(bsz, 6, d)
        w1 = jnp.concatenate([w_in[l], w_gate[l]], axis=1).astype(BF16)
        pad_g = jnp.zeros((d, 8 - MOE_GROUPS), F32)
        pad_e = jnp.zeros((d, ROUTER_COLS - 8 - N_EXPERTS), F32)
        w_router = jnp.concatenate([w_router_group[l], pad_g, w_router_expert[l], pad_e], axis=1)
        w_router_hi = w_router.astype(BF16)
        w_router_lo = (w_router - w_router_hi.astype(F32)).astype(BF16)
        w_router_t = jnp.concatenate([w_router_hi, w_router_lo], axis=1)
        b_router = jnp.concatenate(
            [b_router_group[l], jnp.zeros((8 - MOE_GROUPS,), F32), b_router_expert[l],
             jnp.zeros((ROUTER_COLS - 8 - N_EXPERTS,), F32)])[:, None]
        h1, hn2, eid, rw = _mixer(
            h, ada, g_pre_mix[l][None], g_post_mix[l][None], g_pre_ffn[l][None], w1, b_gate[l][None],
            hg_lb_logits, hg_norm_g[l][None], pool_w[l].astype(BF16), pool_scale[l][None],
            w_branch_hg[l].astype(BF16), w_branch_pool[l].astype(BF16), w_out[l].astype(BF16),
            w_router_t, b_router)

        dest, block_e, meta = _route(eid, n_blocks)
        n_mt = n_tok // MOVE_ROWS
        dest_tiles = dest.reshape(2, n_mt, MOVE_ROWS).transpose(1, 0, 2).reshape(-1)
        xs = _dispatch(dest_tiles, hn2.reshape(n_tok, d // 2), n_blocks * MOE_BLOCK)
        ys = _experts(block_e[0, :n_blocks], meta[0, :1], xs, w_exp_gate[l], w_exp_up[l], w_exp_down[l],
                      n_blocks)
        h = _combine(dest_tiles, ys, rw.T, h1.reshape(n_tok, d), ada, g_post_ffn[l][None], seq)
        h = h.reshape(bsz, seq, d)
    return h
```

```python
import jax
import jax.numpy as jnp
from jax import lax
from jax.experimental import pallas as pl
from jax.experimental.pallas import tpu as pltpu

F32 = jnp.float32
BF16 = jnp.bfloat16
HIGHEST = lax.Precision.HIGHEST

CHUNK = 64
HG_HEADS = 4
POOL_WINDOWS = (2, 4, 8, 16)
POOL_GROUPS = 4
MOE_GROUPS = 4
MOE_EPG = 8
N_EXPERTS = MOE_GROUPS * MOE_EPG
MOE_BLOCK = 256
EPS = 1e-6

MIX_ROWS = 256
MIX_SUBS = 2
POOL_HALO = 128
ROUTE_COLS = 512
MOVE_ROWS = 512
MOVE_UNROLL = 8
ROUTER_COLS = 128
LOG_DECAY_FLOOR = -80.0
VMEM_LIMIT = 56 * 1024 * 1024

NT_DIMS = (((1,), (1,)), ((), ()))
TN_DIMS = (((0,), (0,)), ((), ()))


def _sigmoid(v):
    return 0.5 * jnp.tanh(0.5 * v) + 0.5


def _rms_norm(v, g):
    return v * lax.rsqrt(jnp.mean(v * v, axis=-1, keepdims=True) + EPS) * g


def _pack_bf16_pairs(v):
    n = v.shape[1] // 2
    lo = lax.bitcast_convert_type(v[:, :n].astype(BF16).astype(F32), jnp.uint32)
    hi = lax.bitcast_convert_type(v[:, n:].astype(BF16).astype(F32), jnp.uint32)
    return hi | (lo >> 16)


def _unpack_bf16_pairs(p):
    lo = lax.bitcast_convert_type(p << 16, F32)
    hi = lax.bitcast_convert_type(p & jnp.uint32(0xFFFF0000), F32)
    return jnp.concatenate([lo, hi], axis=1)


def _ada_kernel(c_ref, w_ref, b_ref, o_ref):
    c = c_ref[...]
    cond = c * _sigmoid(c)
    o_ref[...] = jnp.dot(cond, w_ref[...], preferred_element_type=F32, precision=HIGHEST) + b_ref[...]


def _ada(c, w_ada, b_ada):
    bsz, d = c.shape
    n_out = w_ada.shape[1]
    return pl.pallas_call(
        _ada_kernel,
        grid=(n_out // d,),
        in_specs=[pl.BlockSpec((bsz, d), lambda i: (0, 0)),
                  pl.BlockSpec((d, d), lambda i: (0, i)),
                  pl.BlockSpec((1, d), lambda i: (0, i))],
        out_specs=pl.BlockSpec((bsz, d), lambda i: (0, i)),
        out_shape=jax.ShapeDtypeStruct((bsz, n_out), F32),
        name="ada",
    )(c, w_ada, b_ada.reshape(1, n_out))


class _Sub:
    pass


def _mix_kernel(x_ref, ada_ref, gpre_ref, gpost_ref, gffn_ref, w1_ref, bgate_ref, lbl_ref, hgn_ref,
                tril_ref, band_ref, poolw_ref, pools_ref, wbh_ref, wbp_ref, wout_ref, wr_ref, br_ref,
                h1_ref, hn2_ref, eid_ref, rw_ref,
                st_ref, ext_ref, o_scr):
    step_rows, d = x_ref.shape
    t_rows = MIX_ROWS
    hgw = hgn_ref.shape[1]
    hd = hgw // HG_HEADS
    pw = pools_ref.shape[1]
    gd = pw // POOL_GROUPS
    n_chunks = t_rows // CHUNK
    c_u = 4 * hgw
    c_gate = c_u + pw
    j = pl.program_id(1)

    @pl.when(j == 0)
    def _():
        st_ref[...] = jnp.zeros_like(st_ref)
        ext_ref[step_rows:step_rows + POOL_HALO, :] = jnp.zeros((POOL_HALO, pw), BF16)

    ext_ref[0:POOL_HALO, :] = ext_ref[step_rows:step_rows + POOL_HALO, :]

    ada = ada_ref[...]
    sh1, sc1, gt1 = ada[0:1], ada[1:2], ada[2:3]
    sh2, sc2 = ada[3:4], ada[4:5]
    lbl = lbl_ref[...]
    lmax = jnp.maximum(lbl[0:1], lbl[1:2])
    e0 = jnp.exp(lbl[0:1] - lmax)
    lb = e0 / (e0 + jnp.exp(lbl[1:2] - lmax))
    row = lax.broadcasted_iota(jnp.int32, (CHUNK, CHUNK), 0)
    col = lax.broadcasted_iota(jnp.int32, (CHUNK, CHUNK), 1)
    causal = row >= col

    def chunk_heads():
        for ci in range(n_chunks):
            for h in range(HG_HEADS):
                yield ci, h, slice(ci * CHUNK, (ci + 1) * CHUNK), slice(h * hd, (h + 1) * hd)

    def m_project(c):
        x = x_ref[c.rows, :]
        hn = _rms_norm(x, gpre_ref[...]) * (1.0 + sc1) + sh1
        c.proj = jnp.dot(hn.astype(BF16), w1_ref[...], preferred_element_type=F32)

    def e_gates(c):
        qr, fr = c.proj[:, 0:hgw], c.proj[:, hgw:2 * hgw]
        vr, gr = c.proj[:, 2 * hgw:3 * hgw], c.proj[:, 3 * hgw:4 * hgw]
        c.q = qr * _sigmoid(qr)
        f = lb + (1.0 - lb) * _sigmoid(fr)
        c.k = 1.0 - f
        lf = jnp.log(f)
        c.lf_hi = lf.astype(BF16)
        c.lf_lo = (lf - c.lf_hi.astype(F32)).astype(BF16)
        c.vb = vr.astype(BF16)
        c.og = hgn_ref[...] * (gr * _sigmoid(gr))
        c.u = c.proj[:, c_u:c_gate]
        ext_ref[c.ext0 + POOL_HALO:c.ext0 + POOL_HALO + t_rows, :] = c.u.astype(BF16)

    def m_cumsum(c):
        tril = tril_ref[...]
        c.b = (jnp.dot(tril, c.lf_hi, preferred_element_type=F32)
               + jnp.dot(tril, c.lf_lo, preferred_element_type=F32))

    def e_decays(c):
        bc = jnp.maximum(c.b, LOG_DECAY_FLOOR)
        c.qt = (c.q * jnp.exp(bc)).astype(BF16)
        c.kt = (c.k * jnp.exp(-bc)).astype(BF16)
        c.kh, c.dec = [], []
        for ci in range(n_chunks):
            b_c = c.b[ci * CHUNK:(ci + 1) * CHUNK, :]
            b_last = b_c[CHUNK - 1:CHUNK, :]
            c.kh.append((c.k[ci * CHUNK:(ci + 1) * CHUNK, :] * jnp.exp(b_last - b_c)).astype(BF16))
            c.dec.append(jnp.exp(b_last))

    def m_scores(c):
        c.s, c.upd = {}, {}
        for ci, h, rs, cs in chunk_heads():
            c.s[ci, h] = lax.dot_general(c.qt[rs, cs], c.kt[rs, cs], NT_DIMS, preferred_element_type=F32)
            c.upd[ci, h] = lax.dot_general(c.vb[rs, cs], c.kh[ci][:, cs], TN_DIMS,
                                           preferred_element_type=F32)

    def e_states(c):
        c.sb, c.stb = {}, {}
        for ci, h, rs, cs in chunk_heads():
            c.sb[ci, h] = jnp.where(causal, c.s[ci, h], 0.0).astype(BF16)
        for h in range(HG_HEADS):
            st = st_ref[h]
            for ci in range(n_chunks):
                c.stb[ci, h] = st.astype(BF16)
                st = c.dec[ci][:, h * hd:(h + 1) * hd] * st + c.upd[ci, h]
            st_ref[h] = st

    def m_outputs(c):
        c.o = {}
        for ci, h, rs, cs in chunk_heads():
            c.o[ci, h] = (jnp.dot(c.sb[ci, h], c.vb[rs, cs], preferred_element_type=F32)
                          + lax.dot_general(c.qt[rs, cs], c.stb[ci, h], NT_DIMS, preferred_element_type=F32))

    def e_head_norm(c):
        for ci, h, rs, cs in chunk_heads():
            o = c.o[ci, h]
            o = o * lax.rsqrt(jnp.mean(o * o, axis=-1, keepdims=True) + EPS)
            o_scr[c.row0 + ci * CHUNK:c.row0 + (ci + 1) * CHUNK, cs] = (o * c.og[rs, cs]).astype(BF16)

    def m_branch_hg(c):
        c.y_hg = jnp.dot(o_scr[c.rows, :], wbh_ref[...], preferred_element_type=F32)
        ext = ext_ref[c.ext0:c.ext0 + POOL_HALO + t_rows, :]
        c.wsum = [jnp.dot(band_ref[g], ext[:, g * gd:(g + 1) * gd], preferred_element_type=F32)
                  for g in range(POOL_GROUPS)]

    def e_pooled(c):
        pos = j * step_rows + c.row0 + lax.broadcasted_iota(jnp.int32, (t_rows, gd), 0)
        c.pooled = []
        for g in range(POOL_GROUPS):
            cnt = jnp.minimum(pos + 1, POOL_WINDOWS[g]).astype(F32)
            c.pooled.append((c.wsum[g] / cnt - c.u[:, g * gd:(g + 1) * gd]).astype(BF16))

    def m_pool_mix(c):
        c.mixed = [jnp.dot(c.pooled[g], poolw_ref[g], preferred_element_type=F32) for g in range(POOL_GROUPS)]

    def e_gate(c):
        c.mixed = (jnp.concatenate(c.mixed, axis=1) * pools_ref[...]).astype(BF16)
        c.gate = _sigmoid(c.proj[:, c_gate:] + bgate_ref[...])

    def m_branch_pool(c):
        c.y_pool = jnp.dot(c.mixed, wbp_ref[...], preferred_element_type=F32)

    def e_merge(c):
        c.merged = (c.gate[:, 0:d] * c.y_hg + c.gate[:, d:2 * d] * c.y_pool).astype(BF16)

    def m_out(c):
        c.mix = jnp.dot(c.merged, wout_ref[...], preferred_element_type=F32)

    def e_residual(c):
        h1 = x_ref[c.rows, :] + gt1 * _rms_norm(c.mix, gpost_ref[...])
        h1_ref[c.rows, :] = h1
        hn2 = _rms_norm(h1, gffn_ref[...]) * (1.0 + sc2) + sh2
        hn2_ref[c.rows, :] = _pack_bf16_pairs(hn2)
        hn2_hi = hn2.astype(BF16)
        c.hn2_split = jnp.concatenate([hn2_hi, (hn2 - hn2_hi.astype(F32)).astype(BF16)], axis=0)

    def m_router(c):
        c.prod = jnp.dot(c.hn2_split, wr_ref[...], preferred_element_type=F32)

    def e_route(c):
        n_r = wr_ref.shape[1] // 2
        logits = (c.prod[0:t_rows, 0:n_r] + c.prod[t_rows:2 * t_rows, 0:n_r]
                  + c.prod[0:t_rows, n_r:2 * n_r])
        lt = logits.T + br_ref[...]
        lg = lt[0:MOE_GROUPS]
        gmax = jnp.max(lg, axis=0, keepdims=True)
        p_g = 1.0 / jnp.sum(jnp.exp(lg - gmax), axis=0, keepdims=True)
        gi = lax.broadcasted_iota(jnp.int32, lg.shape, 0).astype(F32)
        g_idx = jnp.min(jnp.where(lg == gmax, gi, float(MOE_GROUPS)), axis=0, keepdims=True)
        le = lt[8:8 + MOE_EPG]
        for g in range(1, MOE_GROUPS):
            le = jnp.where(g_idx == float(g), lt[8 + g * MOE_EPG:8 + (g + 1) * MOE_EPG], le)
        ei = lax.broadcasted_iota(jnp.int32, le.shape, 0).astype(F32)
        m1 = jnp.max(le, axis=0, keepdims=True)
        i1 = jnp.min(jnp.where(le == m1, ei, float(MOE_EPG)), axis=0, keepdims=True)
        le2 = jnp.where(ei == i1, -jnp.inf, le)
        m2 = jnp.max(le2, axis=0, keepdims=True)
        i2 = jnp.min(jnp.where(le2 == m2, ei, float(MOE_EPG)), axis=0, keepdims=True)
        r = jnp.exp(m2 - m1)
        w_first = p_g / (1.0 + r)
        eid_ref[0:1, c.rows] = (g_idx * MOE_EPG + i1).astype(jnp.int32)
        eid_ref[1:2, c.rows] = (g_idx * MOE_EPG + i2).astype(jnp.int32)
        rw_ref[0:1, c.rows] = w_first
        rw_ref[1:2, c.rows] = w_first * r

    stages = [m_project, e_gates, m_cumsum, e_decays, m_scores, e_states, m_outputs, e_head_norm,
              m_branch_hg, e_pooled, m_pool_mix, e_gate, m_branch_pool, e_merge, m_out, e_residual,
              m_router, e_route]

    subs = []
    for k in range(step_rows // t_rows):
        c = _Sub()
        c.row0 = k * t_rows
        c.rows = slice(c.row0, c.row0 + t_rows)
        c.ext0 = k * t_rows
        subs.append(c)
    for t in range(len(stages) + len(subs) - 1):
        for k, c in enumerate(subs):
            if 0 <= t - k < len(stages):
                stages[t - k](c)


def _band_matrices(t_rows):
    t = jnp.arange(t_rows)[:, None] + POOL_HALO
    jx = jnp.arange(t_rows + POOL_HALO)[None, :]
    return jnp.stack([((jx <= t) & (jx > t - w)) for w in POOL_WINDOWS]).astype(BF16)


def _chunk_tril(t_rows):
    r = jnp.arange(t_rows)[:, None]
    c = jnp.arange(t_rows)[None, :]
    return ((r >= c) & (r // CHUNK == c // CHUNK)).astype(BF16)


def _mixer(x, ada, g_pre, g_post, g_ffn, w1, b_gate, lb_logits, hg_norm_g, pool_w, pool_scale,
           w_bh, w_bp, w_out, w_router_t, b_router):
    bsz, seq, d = x.shape
    t_rows = MIX_ROWS
    step_rows = MIX_ROWS * MIX_SUBS
    n_t = seq // step_rows
    n_tok = bsz * seq
    hgw = hg_norm_g.shape[1]
    pw = pool_scale.shape[1]
    n_r = b_router.shape[0]

    def const(shape):
        return pl.BlockSpec(shape, lambda b, j: (0,) * len(shape), pipeline_mode=pl.Buffered(1))

    in_specs = [
        pl.BlockSpec((None, step_rows, d), lambda b, j: (b, j, 0)),
        pl.BlockSpec((None, 6, d), lambda b, j: (b, 0, 0)),
        const((1, d)), const((1, d)), const((1, d)),
        const(w1.shape), const((1, 2 * d)),
        const(lb_logits.shape), const((1, hgw)),
        const((t_rows, t_rows)), const((POOL_GROUPS, t_rows, t_rows + POOL_HALO)),
        const(pool_w.shape), const((1, pw)),
        const(w_bh.shape), const(w_bp.shape), const(w_out.shape),
        const(w_router_t.shape), const((n_r, 1)),
    ]
    out_specs = [
        pl.BlockSpec((None, step_rows, d), lambda b, j: (b, j, 0)),
        pl.BlockSpec((None, step_rows, d // 2), lambda b, j: (b, j, 0)),
        pl.BlockSpec((2, step_rows), lambda b, j: (0, b * n_t + j)),
        pl.BlockSpec((2, step_rows), lambda b, j: (0, b * n_t + j)),
    ]
    out_shape = [
        jax.ShapeDtypeStruct((bsz, seq, d), F32),
        jax.ShapeDtypeStruct((bsz, seq, d // 2), jnp.uint32),
        jax.ShapeDtypeStruct((2, n_tok), jnp.int32),
        jax.ShapeDtypeStruct((2, n_tok), F32),
    ]
    scratch = [
        pltpu.VMEM((HG_HEADS, hgw // HG_HEADS, hgw // HG_HEADS), F32),
        pltpu.VMEM((step_rows + POOL_HALO, pw), BF16),
        pltpu.VMEM((step_rows, hgw), BF16),
    ]
    return pl.pallas_call(
        _mix_kernel,
        grid=(bsz, n_t),
        in_specs=in_specs, out_specs=out_specs, out_shape=out_shape, scratch_shapes=scratch,
        compiler_params=pltpu.CompilerParams(dimension_semantics=("arbitrary", "arbitrary"),
                                             vmem_limit_bytes=VMEM_LIMIT),
        name="mixer",
    )(x, ada, g_pre, g_post, g_ffn, w1, b_gate, lb_logits, hg_norm_g, _chunk_tril(t_rows),
      _band_matrices(t_rows), pool_w, pool_scale, w_bh, w_bp, w_out, w_router_t, b_router)


def _route_kernel(eid_ref, dest_ref, etab_ref, cnt_ref, run_ref):
    phase = pl.program_id(0)
    i = pl.program_id(1)
    cols = eid_ref.shape[1]
    eidx = lax.broadcasted_iota(jnp.int32, (N_EXPERTS, cols), 0)
    hot0 = eidx == eid_ref[0:1, :]
    hot1 = eidx == eid_ref[1:2, :]
    both = jnp.where(hot0 | hot1, 1.0, 0.0)
    tile_cnt = jnp.sum(both, axis=1, keepdims=True)

    @pl.when((phase == 0) & (i == 0))
    def _():
        cnt_ref[...] = jnp.zeros_like(cnt_ref)

    @pl.when(phase == 0)
    def _():
        cnt_ref[...] += tile_cnt

    @pl.when((phase == 1) & (i == 0))
    def _():
        nblk = jnp.floor((cnt_ref[...] + float(MOE_BLOCK - 1)) * (1.0 / MOE_BLOCK))
        nblk_f = jnp.broadcast_to(nblk, (N_EXPERTS, 128))
        er = lax.broadcasted_iota(jnp.int32, (N_EXPERTS, N_EXPERTS), 0)
        ec = lax.broadcasted_iota(jnp.int32, (N_EXPERTS, N_EXPERTS), 1)
        lower = jnp.where(ec < er, 1.0, 0.0).astype(BF16)
        start_blk = jnp.dot(lower, nblk_f.astype(BF16), preferred_element_type=F32)
        run_ref[...] = start_blk[:, 0:1] * float(MOE_BLOCK)
        lane = lax.broadcasted_iota(jnp.int32, (N_EXPERTS, 128), 1)
        used = jnp.broadcast_to((start_blk + nblk_f)[N_EXPERTS - 1:N_EXPERTS, :], (N_EXPERTS, 128))
        etab_ref[...] = jnp.where(lane == 0, start_blk,
                                  jnp.where(lane == 1, nblk_f, used)).astype(jnp.int32)

    @pl.when(phase == 1)
    def _():
        r = lax.broadcasted_iota(jnp.int32, (cols, cols), 0)
        c = lax.broadcasted_iota(jnp.int32, (cols, cols), 1)
        before = jnp.where(r < c, 1.0, 0.0).astype(BF16)
        prefix = jnp.dot(both.astype(BF16), before, preferred_element_type=F32)
        slot = run_ref[...] + prefix
        dest_ref[0:1, :] = jnp.sum(jnp.where(hot0, slot, 0.0), axis=0, keepdims=True).astype(jnp.int32)
        dest_ref[1:2, :] = jnp.sum(jnp.where(hot1, slot, 0.0), axis=0, keepdims=True).astype(jnp.int32)
        run_ref[...] += tile_cnt


def _route(eid):
    n_tok = eid.shape[1]
    cols = ROUTE_COLS
    return pl.pallas_call(
        _route_kernel,
        grid=(2, n_tok // cols),
        in_specs=[pl.BlockSpec((2, cols), lambda p, i: (0, i))],
        out_specs=[pl.BlockSpec((2, cols), lambda p, i: (0, i * p)),
                   pl.BlockSpec((N_EXPERTS, 128), lambda p, i: (0, 0))],
        out_shape=[jax.ShapeDtypeStruct((2, n_tok), jnp.int32),
                   jax.ShapeDtypeStruct((N_EXPERTS, 128), jnp.int32)],
        scratch_shapes=[pltpu.VMEM((N_EXPERTS, 1), F32), pltpu.VMEM((N_EXPERTS, 1), F32)],
        compiler_params=pltpu.CompilerParams(dimension_semantics=("arbitrary", "arbitrary")),
        name="route",
    )(eid)


def _row_copy(src_ref, src_row, dst_ref, dst_row, sem):
    return pltpu.make_async_copy(src_ref.at[pl.ds(src_row, 1)], dst_ref.at[pl.ds(dst_row, 1)], sem)


def _drain_rows(rows, wait_pair):
    def drain(g, carry):
        for _ in range(MOVE_UNROLL):
            wait_pair()
        return carry

    lax.fori_loop(0, rows // MOVE_UNROLL, drain, 0)


def _dispatch_kernel(dest_ref, hn2_ref, xs_in_ref, xs_ref, sem):
    del xs_in_ref
    rows = hn2_ref.shape[0]

    def issue(g, carry):
        base = pl.multiple_of(g * MOVE_UNROLL, MOVE_UNROLL)
        group = hn2_ref.at[pl.ds(base, MOVE_UNROLL)]
        for u in range(MOVE_UNROLL):
            _row_copy(group, u, xs_ref, dest_ref[base + u], sem).start(priority=0)
            _row_copy(group, u, xs_ref, dest_ref[rows + base + u], sem).start(priority=1)
        return carry

    lax.fori_loop(0, rows // MOVE_UNROLL, issue, 0)

    def wait_pair():
        _row_copy(hn2_ref, 0, xs_ref, 0, sem).wait()
        _row_copy(hn2_ref, 0, xs_ref, 0, sem).wait()

    _drain_rows(rows, wait_pair)


def _dispatch(dest_tiles, hn2, n_rows_out):
    n_tok, d = hn2.shape
    rows = MOVE_ROWS
    xs0 = jnp.zeros((n_rows_out, d), hn2.dtype)
    return pl.pallas_call(
        _dispatch_kernel,
        grid=(n_tok // rows,),
        in_specs=[pl.BlockSpec((2 * rows,), lambda i: (i,), memory_space=pltpu.SMEM),
                  pl.BlockSpec((rows, d), lambda i: (i, 0)),
                  pl.BlockSpec(memory_space=pl.ANY)],
        out_specs=pl.BlockSpec(memory_space=pl.ANY),
        out_shape=jax.ShapeDtypeStruct((n_rows_out, d), hn2.dtype),
        scratch_shapes=[pltpu.SemaphoreType.DMA],
        input_output_aliases={2: 0},
        compiler_params=pltpu.CompilerParams(dimension_semantics=("arbitrary",)),
        name="dispatch",
    )(dest_tiles, hn2, xs0)


def _expert_kernel(start_ref, nblk_ref, used_ref, wg_ref, wu_ref, wd_ref, xs_ref, ys_ref,
                   wg_s, wu_s, wd_s, xbuf, ybuf, in_sem, out_sem):
    e = pl.program_id(0)
    n = nblk_ref[e]
    first = start_ref[e]
    rows = xbuf.shape[1]
    n_blocks = ys_ref.shape[0] // rows

    def block_rows(blk):
        return pl.ds(pl.multiple_of(blk * rows, rows), rows)

    def in_copy(b, slot):
        return pltpu.make_async_copy(xs_ref.at[block_rows(first + b)], xbuf.at[slot], in_sem.at[slot])

    def out_copy(blk, slot):
        return pltpu.make_async_copy(ybuf.at[slot], ys_ref.at[block_rows(blk)], out_sem.at[slot])

    @pl.when(n > 0)
    def _():
        in_copy(0, 0).start()
        wg_s[...] = wg_ref[...].astype(BF16)
        wu_s[...] = wu_ref[...].astype(BF16)
        wd_s[...] = wd_ref[...].astype(BF16)

        def body(b, carry):
            slot = b % 2
            in_copy(b, slot).wait()

            @pl.when(b + 1 < n)
            def _():
                in_copy(b + 1, 1 - slot).start()

            @pl.when(b >= 2)
            def _():
                out_copy(first + b - 2, slot).wait()

            xb = _unpack_bf16_pairs(xbuf[slot]).astype(BF16)
            gp = jnp.dot(xb, wg_s[...], preferred_element_type=F32)
            up = jnp.dot(xb, wu_s[...], preferred_element_type=F32)
            act = gp * _sigmoid(gp) * up
            ybuf[slot] = _pack_bf16_pairs(jnp.dot(act.astype(BF16), wd_s[...], preferred_element_type=F32))
            out_copy(first + b, slot).start()
            return carry

        lax.fori_loop(0, n, body, 0)

        @pl.when(n >= 2)
        def _():
            out_copy(first + n - 2, n % 2).wait()
        out_copy(first + n - 1, (n - 1) % 2).wait()

    @pl.when(e == pl.num_programs(0) - 1)
    def _():
        ybuf[0] = jnp.zeros(ybuf.shape[1:], ybuf.dtype)

        def fill(blk, carry):
            out_copy(blk, 0).start()
            return carry

        def fill_done(blk, carry):
            out_copy(blk, 0).wait()
            return carry

        lax.fori_loop(used_ref[0], n_blocks, fill, 0)
        lax.fori_loop(used_ref[0], n_blocks, fill_done, 0)


def _experts(first_blk, n_blk, n_used, xs, w_gate, w_up, w_down):
    dp = xs.shape[1]
    n_exp, ff, d = w_down.shape

    def expert_block(e, *_):
        return (e, 0, 0)

    grid_spec = pltpu.PrefetchScalarGridSpec(
        num_scalar_prefetch=3,
        grid=(n_exp,),
        in_specs=[pl.BlockSpec((None, d, ff), expert_block),
                  pl.BlockSpec((None, d, ff), expert_block),
                  pl.BlockSpec((None, ff, d), expert_block),
                  pl.BlockSpec(memory_space=pl.ANY)],
        out_specs=pl.BlockSpec(memory_space=pl.ANY),
        scratch_shapes=[pltpu.VMEM((d, ff), BF16), pltpu.VMEM((d, ff), BF16), pltpu.VMEM((ff, d), BF16),
                        pltpu.VMEM((2, MOE_BLOCK, dp), xs.dtype), pltpu.VMEM((2, MOE_BLOCK, dp), xs.dtype),
                        pltpu.SemaphoreType.DMA((2,)), pltpu.SemaphoreType.DMA((2,))],
    )
    return pl.pallas_call(
        _expert_kernel,
        grid_spec=grid_spec,
        out_shape=jax.ShapeDtypeStruct(xs.shape, xs.dtype),
        compiler_params=pltpu.CompilerParams(dimension_semantics=("arbitrary",),
                                             vmem_limit_bytes=VMEM_LIMIT),
        name="experts",
    )(first_blk, n_blk, n_used, w_gate, w_up, w_down, xs)


def _combine_kernel(dest_ref, ys_ref, rw_ref, h1_ref, ada_ref, g_ref, out_ref, buf0, buf1, sem):
    rows = h1_ref.shape[0]

    def issue(g, carry):
        base = pl.multiple_of(g * MOVE_UNROLL, MOVE_UNROLL)
        group0 = buf0.at[pl.ds(base, MOVE_UNROLL)]
        group1 = buf1.at[pl.ds(base, MOVE_UNROLL)]
        for u in range(MOVE_UNROLL):
            _row_copy(ys_ref, dest_ref[base + u], group0, u, sem).start(priority=0)
            _row_copy(ys_ref, dest_ref[rows + base + u], group1, u, sem).start(priority=1)
        return carry

    lax.fori_loop(0, rows // MOVE_UNROLL, issue, 0)

    def wait_pair():
        _row_copy(ys_ref, 0, buf0, 0, sem).wait()
        _row_copy(ys_ref, 0, buf1, 0, sem).wait()

    _drain_rows(rows, wait_pair)

    rw = rw_ref[...]
    y = rw[:, 0:1] * _unpack_bf16_pairs(buf0[...]) + rw[:, 1:2] * _unpack_bf16_pairs(buf1[...])
    gt2 = ada_ref[5:6, :]
    out_ref[...] = h1_ref[...] + gt2 * _rms_norm(y, g_ref[...])


def _combine(dest_tiles, ys, rw_cols, h1, ada, g_post, seq):
    n_tok, d = h1.shape
    rows = MOVE_ROWS
    per_batch = seq // rows
    return pl.pallas_call(
        _combine_kernel,
        grid=(n_tok // rows,),
        in_specs=[pl.BlockSpec((2 * rows,), lambda i: (i,), memory_space=pltpu.SMEM),
                  pl.BlockSpec(memory_space=pl.ANY),
                  pl.BlockSpec((rows, 2), lambda i: (i, 0)),
                  pl.BlockSpec((rows, d), lambda i: (i, 0)),
                  pl.BlockSpec((None, 6, d), lambda i: (i // per_batch, 0, 0)),
                  pl.BlockSpec((1, d), lambda i: (0, 0))],
        out_specs=pl.BlockSpec((rows, d), lambda i: (i, 0)),
        out_shape=jax.ShapeDtypeStruct((n_tok, d), F32),
        scratch_shapes=[pltpu.VMEM((rows, ys.shape[1]), ys.dtype), pltpu.VMEM((rows, ys.shape[1]), ys.dtype),
                        pltpu.SemaphoreType.DMA],
        compiler_params=pltpu.CompilerParams(dimension_semantics=("arbitrary",)),
        name="combine",
    )(dest_tiles, ys, rw_cols, h1, ada, g_post)


def kernel(x, c, w_ada, b_ada, g_pre_mix, g_post_mix, w_in, hg_lb_logits, hg_norm_g, pool_w, pool_scale,
           w_branch_hg, w_branch_pool, w_gate, b_gate, w_out, g_pre_ffn, g_post_ffn, w_router_group,
           b_router_group, w_router_expert, b_router_expert, w_exp_gate, w_exp_up, w_exp_down):
    depth = w_in.shape[0]
    bsz, seq, d = x.shape
    n_tok = bsz * seq
    n_blocks = -(-(n_tok * 2) // MOE_BLOCK) + N_EXPERTS
    assert seq % (MIX_ROWS * MIX_SUBS) == 0 and MIX_ROWS % CHUNK == 0 and seq % MOVE_ROWS == 0
    assert n_tok % ROUTE_COLS == 0 and hg_lb_logits.shape[0] == 2 and depth == 1

    h = x
    for l in range(depth):
        ada = _ada(c, w_ada[l], b_ada[l]).reshape(bsz, 6, d)
        w1 = jnp.concatenate([w_in[l], w_gate[l]], axis=1).astype(BF16)
        pad_g = jnp.zeros((d, 8 - MOE_GROUPS), F32)
        pad_e = jnp.zeros((d, ROUTER_COLS - 8 - N_EXPERTS), F32)
        w_router = jnp.concatenate([w_router_group[l], pad_g, w_router_expert[l], pad_e], axis=1)
        w_router_hi = w_router.astype(BF16)
        w_router_lo = (w_router - w_router_hi.astype(F32)).astype(BF16)
        w_router_t = jnp.concatenate([w_router_hi, w_router_lo], axis=1)
        b_router = jnp.concatenate(
            [b_router_group[l], jnp.zeros((8 - MOE_GROUPS,), F32), b_router_expert[l],
             jnp.zeros((ROUTER_COLS - 8 - N_EXPERTS,), F32)])[:, None]
        h1, hn2, eid, rw = _mixer(
            h, ada, g_pre_mix[l][None], g_post_mix[l][None], g_pre_ffn[l][None], w1, b_gate[l][None],
            hg_lb_logits, hg_norm_g[l][None], pool_w[l].astype(BF16), pool_scale[l][None],
            w_branch_hg[l].astype(BF16), w_branch_pool[l].astype(BF16), w_out[l].astype(BF16),
            w_router_t, b_router)

        dest, etab = _route(eid)
        n_mt = n_tok // MOVE_ROWS
        dest_tiles = dest.reshape(2, n_mt, MOVE_ROWS).transpose(1, 0, 2).reshape(-1)
        xs = _dispatch(dest_tiles, hn2.reshape(n_tok, d // 2), n_blocks * MOE_BLOCK)
        ys = _experts(etab[:, 0], etab[:, 1], etab[:1, 2], xs, w_exp_gate[l], w_exp_up[l], w_exp_down[l])
        h = _combine(dest_tiles, ys, rw.T, h1.reshape(n_tok, d), ada, g_post_ffn[l][None], seq)
        h = h.reshape(bsz, seq, d)
    return h
```

```python
import jax
import jax.numpy as jnp
from jax import lax
from jax.experimental import pallas as pl
from jax.experimental.pallas import tpu as pltpu

F32 = jnp.float32
BF16 = jnp.bfloat16
HIGHEST = lax.Precision.HIGHEST

CHUNK = 64
HG_HEADS = 4
POOL_WINDOWS = (2, 4, 8, 16)
POOL_GROUPS = 4
MOE_GROUPS = 4
MOE_EPG = 8
N_EXPERTS = MOE_GROUPS * MOE_EPG
MOE_BLOCK = 256
EPS = 1e-6

MIX_ROWS = 256
MIX_SUBS = 2
POOL_HALO = 128
ROUTE_COLS = 512
MOVE_ROWS = 512
MOVE_UNROLL = 8
ROUTER_COLS = 128
LOG_DECAY_FLOOR = -80.0
VMEM_LIMIT = 56 * 1024 * 1024

NT_DIMS = (((1,), (1,)), ((), ()))
TN_DIMS = (((0,), (0,)), ((), ()))


def _sigmoid(v):
    return 0.5 * jnp.tanh(0.5 * v) + 0.5


def _rms_norm(v, g):
    return v * lax.rsqrt(jnp.mean(v * v, axis=-1, keepdims=True) + EPS) * g


def _pack_bf16_pairs(v):
    n = v.shape[1] // 2
    lo = lax.bitcast_convert_type(v[:, :n].astype(BF16).astype(F32), jnp.uint32)
    hi = lax.bitcast_convert_type(v[:, n:].astype(BF16).astype(F32), jnp.uint32)
    return hi | (lo >> 16)


def _unpack_bf16_pairs(p):
    lo = lax.bitcast_convert_type(p << 16, F32)
    hi = lax.bitcast_convert_type(p & jnp.uint32(0xFFFF0000), F32)
    return jnp.concatenate([lo, hi], axis=1)


def _ada_kernel(c_ref, w_ref, b_ref, o_ref):
    c = c_ref[...]
    cond = c * _sigmoid(c)
    o_ref[...] = jnp.dot(cond, w_ref[...], preferred_element_type=F32, precision=HIGHEST) + b_ref[...]


def _ada(c, w_ada, b_ada):
    bsz, d = c.shape
    n_out = w_ada.shape[1]
    return pl.pallas_call(
        _ada_kernel,
        grid=(n_out // d,),
        in_specs=[pl.BlockSpec((bsz, d), lambda i: (0, 0)),
                  pl.BlockSpec((d, d), lambda i: (0, i)),
                  pl.BlockSpec((1, d), lambda i: (0, i))],
        out_specs=pl.BlockSpec((bsz, d), lambda i: (0, i)),
        out_shape=jax.ShapeDtypeStruct((bsz, n_out), F32),
        name="ada",
    )(c, w_ada, b_ada.reshape(1, n_out))


class _Sub:
    pass


def _mix_kernel(x_ref, ada_ref, gpre_ref, gpost_ref, gffn_ref, w1_ref, bgate_ref, lbl_ref, hgn_ref,
                tril_ref, band_ref, poolw_ref, pools_ref, wbh_ref, wbp_ref, wout_ref, wr_ref, br_ref,
                h1_ref, hn2_ref, eid_ref, rw_ref,
                st_ref, ext_ref, o_scr):
    step_rows, d = x_ref.shape
    t_rows = MIX_ROWS
    hgw = hgn_ref.shape[1]
    hd = hgw // HG_HEADS
    pw = pools_ref.shape[1]
    gd = pw // POOL_GROUPS
    n_chunks = t_rows // CHUNK
    c_u = 4 * hgw
    c_gate = c_u + pw
    j = pl.program_id(1)

    @pl.when(j == 0)
    def _():
        st_ref[...] = jnp.zeros_like(st_ref)
        ext_ref[step_rows:step_rows + POOL_HALO, :] = jnp.zeros((POOL_HALO, pw), BF16)

    ext_ref[0:POOL_HALO, :] = ext_ref[step_rows:step_rows + POOL_HALO, :]

    ada = ada_ref[...]
    sh1, sc1, gt1 = ada[0:1], ada[1:2], ada[2:3]
    sh2, sc2 = ada[3:4], ada[4:5]
    lbl = lbl_ref[...]
    lmax = jnp.maximum(lbl[0:1], lbl[1:2])
    e0 = jnp.exp(lbl[0:1] - lmax)
    lb = e0 / (e0 + jnp.exp(lbl[1:2] - lmax))
    row = lax.broadcasted_iota(jnp.int32, (CHUNK, CHUNK), 0)
    col = lax.broadcasted_iota(jnp.int32, (CHUNK, CHUNK), 1)
    causal = row >= col

    def chunk_heads():
        for ci in range(n_chunks):
            for h in range(HG_HEADS):
                yield ci, h, slice(ci * CHUNK, (ci + 1) * CHUNK), slice(h * hd, (h + 1) * hd)

    def m_project(c):
        x = x_ref[c.rows, :]
        hn = _rms_norm(x, gpre_ref[...]) * (1.0 + sc1) + sh1
        c.proj = jnp.dot(hn.astype(BF16), w1_ref[...], preferred_element_type=F32)

    def e_gates(c):
        qr, fr = c.proj[:, 0:hgw], c.proj[:, hgw:2 * hgw]
        vr, gr = c.proj[:, 2 * hgw:3 * hgw], c.proj[:, 3 * hgw:4 * hgw]
        c.q = qr * _sigmoid(qr)
        f = lb + (1.0 - lb) * _sigmoid(fr)
        c.k = 1.0 - f
        lf = jnp.log(f)
        c.lf_hi = lf.astype(BF16)
        c.lf_lo = (lf - c.lf_hi.astype(F32)).astype(BF16)
        c.vb = vr.astype(BF16)
        c.og = hgn_ref[...] * (gr * _sigmoid(gr))
        c.u = c.proj[:, c_u:c_gate]
        ext_ref[c.ext0 + POOL_HALO:c.ext0 + POOL_HALO + t_rows, :] = c.u.astype(BF16)

    def m_cumsum(c):
        tril = tril_ref[...]
        c.b = (jnp.dot(tril, c.lf_hi, preferred_element_type=F32)
               + jnp.dot(tril, c.lf_lo, preferred_element_type=F32))

    def e_decays(c):
        bc = jnp.maximum(c.b, LOG_DECAY_FLOOR)
        c.qt = (c.q * jnp.exp(bc)).astype(BF16)
        c.kt = (c.k * jnp.exp(-bc)).astype(BF16)
        c.kh, c.dec = [], []
        for ci in range(n_chunks):
            b_c = c.b[ci * CHUNK:(ci + 1) * CHUNK, :]
            b_last = b_c[CHUNK - 1:CHUNK, :]
            c.kh.append((c.k[ci * CHUNK:(ci + 1) * CHUNK, :] * jnp.exp(b_last - b_c)).astype(BF16))
            c.dec.append(jnp.exp(b_last))

    def m_scores(c):
        c.s, c.upd = {}, {}
        for ci, h, rs, cs in chunk_heads():
            c.s[ci, h] = lax.dot_general(c.qt[rs, cs], c.kt[rs, cs], NT_DIMS, preferred_element_type=F32)
            c.upd[ci, h] = lax.dot_general(c.vb[rs, cs], c.kh[ci][:, cs], TN_DIMS,
                                           preferred_element_type=F32)

    def e_states(c):
        c.sb, c.stb = {}, {}
        for ci, h, rs, cs in chunk_heads():
            c.sb[ci, h] = jnp.where(causal, c.s[ci, h], 0.0).astype(BF16)
        for h in range(HG_HEADS):
            st = st_ref[h]
            for ci in range(n_chunks):
                c.stb[ci, h] = st.astype(BF16)
                st = c.dec[ci][:, h * hd:(h + 1) * hd] * st + c.upd[ci, h]
            st_ref[h] = st

    def m_outputs(c):
        c.o = {}
        for ci, h, rs, cs in chunk_heads():
            c.o[ci, h] = (jnp.dot(c.sb[ci, h], c.vb[rs, cs], preferred_element_type=F32)
                          + lax.dot_general(c.qt[rs, cs], c.stb[ci, h], NT_DIMS, preferred_element_type=F32))

    def e_head_norm(c):
        for ci, h, rs, cs in chunk_heads():
            o = c.o[ci, h]
            o = o * lax.rsqrt(jnp.mean(o * o, axis=-1, keepdims=True) + EPS)
            o_scr[c.row0 + ci * CHUNK:c.row0 + (ci + 1) * CHUNK, cs] = (o * c.og[rs, cs]).astype(BF16)

    def m_branch_hg(c):
        c.y_hg = jnp.dot(o_scr[c.rows, :], wbh_ref[...], preferred_element_type=F32)
        ext = ext_ref[c.ext0:c.ext0 + POOL_HALO + t_rows, :]
        c.wsum = [jnp.dot(band_ref[g], ext[:, g * gd:(g + 1) * gd], preferred_element_type=F32)
                  for g in range(POOL_GROUPS)]

    def e_pooled(c):
        pos = j * step_rows + c.row0 + lax.broadcasted_iota(jnp.int32, (t_rows, gd), 0)
        c.pooled = []
        for g in range(POOL_GROUPS):
            cnt = jnp.minimum(pos + 1, POOL_WINDOWS[g]).astype(F32)
            c.pooled.append((c.wsum[g] / cnt - c.u[:, g * gd:(g + 1) * gd]).astype(BF16))

    def m_pool_mix(c):
        c.mixed = [jnp.dot(c.pooled[g], poolw_ref[g], preferred_element_type=F32) for g in range(POOL_GROUPS)]

    def e_gate(c):
        c.mixed = (jnp.concatenate(c.mixed, axis=1) * pools_ref[...]).astype(BF16)
        c.gate = _sigmoid(c.proj[:, c_gate:] + bgate_ref[...])

    def m_branch_pool(c):
        c.y_pool = jnp.dot(c.mixed, wbp_ref[...], preferred_element_type=F32)

    def e_merge(c):
        c.merged = (c.gate[:, 0:d] * c.y_hg + c.gate[:, d:2 * d] * c.y_pool).astype(BF16)

    def m_out(c):
        c.mix = jnp.dot(c.merged, wout_ref[...], preferred_element_type=F32)

    def e_residual(c):
        h1 = x_ref[c.rows, :] + gt1 * _rms_norm(c.mix, gpost_ref[...])
        h1_ref[c.rows, :] = h1
        hn2 = _rms_norm(h1, gffn_ref[...]) * (1.0 + sc2) + sh2
        hn2_ref[c.rows, :] = _pack_bf16_pairs(hn2)
        hn2_hi = hn2.astype(BF16)
        c.hn2_split = jnp.concatenate([hn2_hi, (hn2 - hn2_hi.astype(F32)).astype(BF16)], axis=0)

    def m_router(c):
        c.prod = jnp.dot(c.hn2_split, wr_ref[...], preferred_element_type=F32)

    def e_route(c):
        n_r = wr_ref.shape[1] // 2
        logits = (c.prod[0:t_rows, 0:n_r] + c.prod[t_rows:2 * t_rows, 0:n_r]
                  + c.prod[0:t_rows, n_r:2 * n_r])
        lt = logits.T + br_ref[...]
        lg = lt[0:MOE_GROUPS]
        gmax = jnp.max(lg, axis=0, keepdims=True)
        p_g = 1.0 / jnp.sum(jnp.exp(lg - gmax), axis=0, keepdims=True)
        gi = lax.broadcasted_iota(jnp.int32, lg.shape, 0).astype(F32)
        g_idx = jnp.min(jnp.where(lg == gmax, gi, float(MOE_GROUPS)), axis=0, keepdims=True)
        le = lt[8:8 + MOE_EPG]
        for g in range(1, MOE_GROUPS):
            le = jnp.where(g_idx == float(g), lt[8 + g * MOE_EPG:8 + (g + 1) * MOE_EPG], le)
        ei = lax.broadcasted_iota(jnp.int32, le.shape, 0).astype(F32)
        m1 = jnp.max(le, axis=0, keepdims=True)
        i1 = jnp.min(jnp.where(le == m1, ei, float(MOE_EPG)), axis=0, keepdims=True)
        le2 = jnp.where(ei == i1, -jnp.inf, le)
        m2 = jnp.max(le2, axis=0, keepdims=True)
        i2 = jnp.min(jnp.where(le2 == m2, ei, float(MOE_EPG)), axis=0, keepdims=True)
        r = jnp.exp(m2 - m1)
        w_first = p_g / (1.0 + r)
        eid_ref[0:1, c.rows] = (g_idx * MOE_EPG + i1).astype(jnp.int32)
        eid_ref[1:2, c.rows] = (g_idx * MOE_EPG + i2).astype(jnp.int32)
        rw_ref[0:1, c.rows] = w_first
        rw_ref[1:2, c.rows] = w_first * r

    stages = [m_project, e_gates, m_cumsum, e_decays, m_scores, e_states, m_outputs, e_head_norm,
              m_branch_hg, e_pooled, m_pool_mix, e_gate, m_branch_pool, e_merge, m_out, e_residual,
              m_router, e_route]

    subs = []
    for k in range(step_rows // t_rows):
        c = _Sub()
        c.row0 = k * t_rows
        c.rows = slice(c.row0, c.row0 + t_rows)
        c.ext0 = k * t_rows
        subs.append(c)
    for t in range(len(stages) + len(subs) - 1):
        for k, c in enumerate(subs):
            if 0 <= t - k < len(stages):
                stages[t - k](c)


def _band_matrices(t_rows):
    t = jnp.arange(t_rows)[:, None] + POOL_HALO
    jx = jnp.arange(t_rows + POOL_HALO)[None, :]
    return jnp.stack([((jx <= t) & (jx > t - w)) for w in POOL_WINDOWS]).astype(BF16)


def _chunk_tril(t_rows):
    r = jnp.arange(t_rows)[:, None]
    c = jnp.arange(t_rows)[None, :]
    return ((r >= c) & (r // CHUNK == c // CHUNK)).astype(BF16)


def _mixer(x, ada, g_pre, g_post, g_ffn, w1, b_gate, lb_logits, hg_norm_g, pool_w, pool_scale,
           w_bh, w_bp, w_out, w_router_t, b_router):
    bsz, seq, d = x.shape
    t_rows = MIX_ROWS
    step_rows = MIX_ROWS * MIX_SUBS
    n_t = seq // step_rows
    n_tok = bsz * seq
    hgw = hg_norm_g.shape[1]
    pw = pool_scale.shape[1]
    n_r = b_router.shape[0]

    def const(shape):
        return pl.BlockSpec(shape, lambda b, j: (0,) * len(shape), pipeline_mode=pl.Buffered(1))

    in_specs = [
        pl.BlockSpec((None, step_rows, d), lambda b, j: (b, j, 0)),
        pl.BlockSpec((None, 6, d), lambda b, j: (b, 0, 0)),
        const((1, d)), const((1, d)), const((1, d)),
        const(w1.shape), const((1, 2 * d)),
        const(lb_logits.shape), const((1, hgw)),
        const((t_rows, t_rows)), const((POOL_GROUPS, t_rows, t_rows + POOL_HALO)),
        const(pool_w.shape), const((1, pw)),
        const(w_bh.shape), const(w_bp.shape), const(w_out.shape),
        const(w_router_t.shape), const((n_r, 1)),
    ]
    out_specs = [
        pl.BlockSpec((None, step_rows, d), lambda b, j: (b, j, 0)),
        pl.BlockSpec((None, step_rows, d // 2), lambda b, j: (b, j, 0)),
        pl.BlockSpec((2, step_rows), lambda b, j: (0, b * n_t + j)),
        pl.BlockSpec((2, step_rows), lambda b, j: (0, b * n_t + j)),
    ]
    out_shape = [
        jax.ShapeDtypeStruct((bsz, seq, d), F32),
        jax.ShapeDtypeStruct((bsz, seq, d // 2), jnp.uint32),
        jax.ShapeDtypeStruct((2, n_tok), jnp.int32),
        jax.ShapeDtypeStruct((2, n_tok), F32),
    ]
    scratch = [
        pltpu.VMEM((HG_HEADS, hgw // HG_HEADS, hgw // HG_HEADS), F32),
        pltpu.VMEM((step_rows + POOL_HALO, pw), BF16),
        pltpu.VMEM((step_rows, hgw), BF16),
    ]
    return pl.pallas_call(
        _mix_kernel,
        grid=(bsz, n_t),
        in_specs=in_specs, out_specs=out_specs, out_shape=out_shape, scratch_shapes=scratch,
        compiler_params=pltpu.CompilerParams(dimension_semantics=("arbitrary", "arbitrary"),
                                             vmem_limit_bytes=VMEM_LIMIT),
        name="mixer",
    )(x, ada, g_pre, g_post, g_ffn, w1, b_gate, lb_logits, hg_norm_g, _chunk_tril(t_rows),
      _band_matrices(t_rows), pool_w, pool_scale, w_bh, w_bp, w_out, w_router_t, b_router)


def _route_kernel(eid_ref, dest_ref, etab_ref, cnt_ref, run_ref):
    phase = pl.program_id(0)
    i = pl.program_id(1)
    cols = eid_ref.shape[1]
    eidx = lax.broadcasted_iota(jnp.int32, (N_EXPERTS, cols), 0)
    hot0 = eidx == eid_ref[0:1, :]
    hot1 = eidx == eid_ref[1:2, :]
    both = jnp.where(hot0 | hot1, 1.0, 0.0)
    tile_cnt = jnp.sum(both, axis=1, keepdims=True)

    @pl.when((phase == 0) & (i == 0))
    def _():
        cnt_ref[...] = jnp.zeros_like(cnt_ref)

    @pl.when(phase == 0)
    def _():
        cnt_ref[...] += tile_cnt

    @pl.when((phase == 1) & (i == 0))
    def _():
        nblk = jnp.floor((cnt_ref[...] + float(MOE_BLOCK - 1)) * (1.0 / MOE_BLOCK))
        nblk_f = jnp.broadcast_to(nblk, (N_EXPERTS, 128))
        er = lax.broadcasted_iota(jnp.int32, (N_EXPERTS, N_EXPERTS), 0)
        ec = lax.broadcasted_iota(jnp.int32, (N_EXPERTS, N_EXPERTS), 1)
        lower = jnp.where(ec < er, 1.0, 0.0).astype(BF16)
        start_blk = jnp.dot(lower, nblk_f.astype(BF16), preferred_element_type=F32)
        run_ref[...] = start_blk[:, 0:1] * float(MOE_BLOCK)
        lane = lax.broadcasted_iota(jnp.int32, (N_EXPERTS, 128), 1)
        used = jnp.broadcast_to((start_blk + nblk_f)[N_EXPERTS - 1:N_EXPERTS, :], (N_EXPERTS, 128))
        etab_ref[...] = jnp.where(lane == 0, start_blk,
                                  jnp.where(lane == 1, nblk_f, used)).astype(jnp.int32)

    @pl.when(phase == 1)
    def _():
        r = lax.broadcasted_iota(jnp.int32, (cols, cols), 0)
        c = lax.broadcasted_iota(jnp.int32, (cols, cols), 1)
        before = jnp.where(r < c, 1.0, 0.0).astype(BF16)
        prefix = jnp.dot(both.astype(BF16), before, preferred_element_type=F32)
        slot = run_ref[...] + prefix
        dest_ref[0:1, :] = jnp.sum(jnp.where(hot0, slot, 0.0), axis=0, keepdims=True).astype(jnp.int32)
        dest_ref[1:2, :] = jnp.sum(jnp.where(hot1, slot, 0.0), axis=0, keepdims=True).astype(jnp.int32)
        run_ref[...] += tile_cnt


def _route(eid):
    n_tok = eid.shape[1]
    cols = ROUTE_COLS
    return pl.pallas_call(
        _route_kernel,
        grid=(2, n_tok // cols),
        in_specs=[pl.BlockSpec((2, cols), lambda p, i: (0, i))],
        out_specs=[pl.BlockSpec((2, cols), lambda p, i: (0, i * p)),
                   pl.BlockSpec((N_EXPERTS, 128), lambda p, i: (0, 0))],
        out_shape=[jax.ShapeDtypeStruct((2, n_tok), jnp.int32),
                   jax.ShapeDtypeStruct((N_EXPERTS, 128), jnp.int32)],
        scratch_shapes=[pltpu.VMEM((N_EXPERTS, 1), F32), pltpu.VMEM((N_EXPERTS, 1), F32)],
        compiler_params=pltpu.CompilerParams(dimension_semantics=("arbitrary", "arbitrary")),
        name="route",
    )(eid)


def _row_copy(src_ref, src_row, dst_ref, dst_row, sem):
    return pltpu.make_async_copy(src_ref.at[pl.ds(src_row, 1)], dst_ref.at[pl.ds(dst_row, 1)], sem)


def _drain_rows(rows, wait_pair):
    def drain(g, carry):
        for _ in range(MOVE_UNROLL):
            wait_pair()
        return carry

    lax.fori_loop(0, rows // MOVE_UNROLL, drain, 0)


def _dispatch_kernel(dest_ref, hn2_ref, xs_in_ref, xs_ref, sem):
    del xs_in_ref
    rows = hn2_ref.shape[0]

    def issue(g, carry):
        base = pl.multiple_of(g * MOVE_UNROLL, MOVE_UNROLL)
        group = hn2_ref.at[pl.ds(base, MOVE_UNROLL)]
        for u in range(MOVE_UNROLL):
            _row_copy(group, u, xs_ref, dest_ref[base + u], sem).start(priority=0)
            _row_copy(group, u, xs_ref, dest_ref[rows + base + u], sem).start(priority=1)
        return carry

    lax.fori_loop(0, rows // MOVE_UNROLL, issue, 0)

    def wait_pair():
        _row_copy(hn2_ref, 0, xs_ref, 0, sem).wait()
        _row_copy(hn2_ref, 0, xs_ref, 0, sem).wait()

    _drain_rows(rows, wait_pair)


def _dispatch(dest_tiles, hn2, n_rows_out):
    n_tok, d = hn2.shape
    rows = MOVE_ROWS
    xs0 = jnp.zeros((n_rows_out, d), hn2.dtype)
    return pl.pallas_call(
        _dispatch_kernel,
        grid=(n_tok // rows,),
        in_specs=[pl.BlockSpec((2 * rows,), lambda i: (i,), memory_space=pltpu.SMEM),
                  pl.BlockSpec((rows, d), lambda i: (i, 0)),
                  pl.BlockSpec(memory_space=pl.ANY)],
        out_specs=pl.BlockSpec(memory_space=pl.ANY),
        out_shape=jax.ShapeDtypeStruct((n_rows_out, d), hn2.dtype),
        scratch_shapes=[pltpu.SemaphoreType.DMA],
        input_output_aliases={2: 0},
        compiler_params=pltpu.CompilerParams(dimension_semantics=("arbitrary",)),
        name="dispatch",
    )(dest_tiles, hn2, xs0)


def _expert_kernel(start_ref, nblk_ref, used_ref, wg_ref, wu_ref, wd_ref, xs_ref, ys_ref,
                   wg_s, wu_s, wd_s, xbuf, ybuf, in_sem, out_sem):
    e = pl.program_id(0)
    n = nblk_ref[e]
    first = start_ref[e]
    rows = xbuf.shape[1]
    n_blocks = ys_ref.shape[0] // rows

    def block_rows(blk):
        return pl.ds(pl.multiple_of(blk * rows, rows), rows)

    def in_copy(b, slot):
        return pltpu.make_async_copy(xs_ref.at[block_rows(first + b)], xbuf.at[slot], in_sem.at[slot])

    def out_copy(blk, slot):
        return pltpu.make_async_copy(ybuf.at[slot], ys_ref.at[block_rows(blk)], out_sem.at[slot])

    @pl.when(n > 0)
    def _():
        in_copy(0, 0).start(priority=1)
        wg_s[...] = wg_ref[...].astype(BF16)
        wu_s[...] = wu_ref[...].astype(BF16)
        wd_s[...] = wd_ref[...].astype(BF16)

        def body(b, carry):
            slot = b % 2
            in_copy(b, slot).wait()

            @pl.when(b + 1 < n)
            def _():
                in_copy(b + 1, 1 - slot).start(priority=1)

            @pl.when(b >= 2)
            def _():
                out_copy(first + b - 2, slot).wait()

            xb = _unpack_bf16_pairs(xbuf[slot]).astype(BF16)
            gp = jnp.dot(xb, wg_s[...], preferred_element_type=F32)
            up = jnp.dot(xb, wu_s[...], preferred_element_type=F32)
            act = gp * _sigmoid(gp) * up
            ybuf[slot] = _pack_bf16_pairs(jnp.dot(act.astype(BF16), wd_s[...], preferred_element_type=F32))
            out_copy(first + b, slot).start(priority=1)
            return carry

        lax.fori_loop(0, n, body, 0)

        @pl.when(n >= 2)
        def _():
            out_copy(first + n - 2, n % 2).wait()
        out_copy(first + n - 1, (n - 1) % 2).wait()

    @pl.when(e == pl.num_programs(0) - 1)
    def _():
        ybuf[0] = jnp.zeros(ybuf.shape[1:], ybuf.dtype)

        def fill(blk, carry):
            out_copy(blk, 0).start()
            return carry

        def fill_done(blk, carry):
            out_copy(blk, 0).wait()
            return carry

        lax.fori_loop(used_ref[0], n_blocks, fill, 0)
        lax.fori_loop(used_ref[0], n_blocks, fill_done, 0)


def _experts(first_blk, n_blk, n_used, xs, w_gate, w_up, w_down):
    dp = xs.shape[1]
    n_exp, ff, d = w_down.shape

    def expert_block(e, *_):
        return (e, 0, 0)

    grid_spec = pltpu.PrefetchScalarGridSpec(
        num_scalar_prefetch=3,
        grid=(n_exp,),
        in_specs=[pl.BlockSpec((None, d, ff), expert_block),
                  pl.BlockSpec((None, d, ff), expert_block),
                  pl.BlockSpec((None, ff, d), expert_block),
                  pl.BlockSpec(memory_space=pl.ANY)],
        out_specs=pl.BlockSpec(memory_space=pl.ANY),
        scratch_shapes=[pltpu.VMEM((d, ff), BF16), pltpu.VMEM((d, ff), BF16), pltpu.VMEM((ff, d), BF16),
                        pltpu.VMEM((2, MOE_BLOCK, dp), xs.dtype), pltpu.VMEM((2, MOE_BLOCK, dp), xs.dtype),
                        pltpu.SemaphoreType.DMA((2,)), pltpu.SemaphoreType.DMA((2,))],
    )
    return pl.pallas_call(
        _expert_kernel,
        grid_spec=grid_spec,
        out_shape=jax.ShapeDtypeStruct(xs.shape, xs.dtype),
        compiler_params=pltpu.CompilerParams(dimension_semantics=("arbitrary",),
                                             vmem_limit_bytes=VMEM_LIMIT),
        name="experts",
    )(first_blk, n_blk, n_used, w_gate, w_up, w_down, xs)


def _combine_kernel(dest_ref, ys_ref, rw_ref, h1_ref, ada_ref, g_ref, out_ref, buf0, buf1, sem):
    rows = h1_ref.shape[0]

    def issue(g, carry):
        base = pl.multiple_of(g * MOVE_UNROLL, MOVE_UNROLL)
        group0 = buf0.at[pl.ds(base, MOVE_UNROLL)]
        group1 = buf1.at[pl.ds(base, MOVE_UNROLL)]
        for u in range(MOVE_UNROLL):
            _row_copy(ys_ref, dest_ref[base + u], group0, u, sem).start(priority=0)
            _row_copy(ys_ref, dest_ref[rows + base + u], group1, u, sem).start(priority=1)
        return carry

    lax.fori_loop(0, rows // MOVE_UNROLL, issue, 0)

    def wait_pair():
        _row_copy(ys_ref, 0, buf0, 0, sem).wait()
        _row_copy(ys_ref, 0, buf1, 0, sem).wait()

    _drain_rows(rows, wait_pair)

    rw = rw_ref[...]
    y = rw[:, 0:1] * _unpack_bf16_pairs(buf0[...]) + rw[:, 1:2] * _unpack_bf16_pairs(buf1[...])
    gt2 = ada_ref[5:6, :]
    out_ref[...] = h1_ref[...] + gt2 * _rms_norm(y, g_ref[...])


def _combine(dest_tiles, ys, rw_cols, h1, ada, g_post, seq):
    n_tok, d = h1.shape
    rows = MOVE_ROWS
    per_batch = seq // rows
    return pl.pallas_call(
        _combine_kernel,
        grid=(n_tok // rows,),
        in_specs=[pl.BlockSpec((2 * rows,), lambda i: (i,), memory_space=pltpu.SMEM),
                  pl.BlockSpec(memory_space=pl.ANY),
                  pl.BlockSpec((rows, 2), lambda i: (i, 0)),
                  pl.BlockSpec((rows, d), lambda i: (i, 0)),
                  pl.BlockSpec((None, 6, d), lambda i: (i // per_batch, 0, 0)),
                  pl.BlockSpec((1, d), lambda i: (0, 0))],
        out_specs=pl.BlockSpec((rows, d), lambda i: (i, 0)),
        out_shape=jax.ShapeDtypeStruct((n_tok, d), F32),
        scratch_shapes=[pltpu.VMEM((rows, ys.shape[1]), ys.dtype), pltpu.VMEM((rows, ys.shape[1]), ys.dtype),
                        pltpu.SemaphoreType.DMA],
        compiler_params=pltpu.CompilerParams(dimension_semantics=("arbitrary",)),
        name="combine",
    )(dest_tiles, ys, rw_cols, h1, ada, g_post)


def kernel(x, c, w_ada, b_ada, g_pre_mix, g_post_mix, w_in, hg_lb_logits, hg_norm_g, pool_w, pool_scale,
           w_branch_hg, w_branch_pool, w_gate, b_gate, w_out, g_pre_ffn, g_post_ffn, w_router_group,
           b_router_group, w_router_expert, b_router_expert, w_exp_gate, w_exp_up, w_exp_down):
    depth = w_in.shape[0]
    bsz, seq, d = x.shape
    n_tok = bsz * seq
    n_blocks = -(-(n_tok * 2) // MOE_BLOCK) + N_EXPERTS
    assert seq % (MIX_ROWS * MIX_SUBS) == 0 and MIX_ROWS % CHUNK == 0 and seq % MOVE_ROWS == 0
    assert n_tok % ROUTE_COLS == 0 and hg_lb_logits.shape[0] == 2 and depth == 1

    h = x
    for l in range(depth):
        ada = _ada(c, w_ada[l], b_ada[l]).reshape(bsz, 6, d)
        w1 = jnp.concatenate([w_in[l], w_gate[l]], axis=1).astype(BF16)
        pad_g = jnp.zeros((d, 8 - MOE_GROUPS), F32)
        pad_e = jnp.zeros((d, ROUTER_COLS - 8 - N_EXPERTS), F32)
        w_router = jnp.concatenate([w_router_group[l], pad_g, w_router_expert[l], pad_e], axis=1)
        w_router_hi = w_router.astype(BF16)
        w_router_lo = (w_router - w_router_hi.astype(F32)).astype(BF16)
        w_router_t = jnp.concatenate([w_router_hi, w_router_lo], axis=1)
        b_router = jnp.concatenate(
            [b_router_group[l], jnp.zeros((8 - MOE_GROUPS,), F32), b_router_expert[l],
             jnp.zeros((ROUTER_COLS - 8 - N_EXPERTS,), F32)])[:, None]
        h1, hn2, eid, rw = _mixer(
            h, ada, g_pre_mix[l][None], g_post_mix[l][None], g_pre_ffn[l][None], w1, b_gate[l][None],
            hg_lb_logits, hg_norm_g[l][None], pool_w[l].astype(BF16), pool_scale[l][None],
            w_branch_hg[l].astype(BF16), w_branch_pool[l].astype(BF16), w_out[l].astype(BF16),
            w_router_t, b_router)

        dest, etab = _route(eid)
        n_mt = n_tok // MOVE_ROWS
        dest_tiles = dest.reshape(2, n_mt, MOVE_ROWS).transpose(1, 0, 2).reshape(-1)
        xs = _dispatch(dest_tiles, hn2.reshape(n_tok, d // 2), n_blocks * MOE_BLOCK)
        ys = _experts(etab[:, 0], etab[:, 1], etab[:1, 2], xs, w_exp_gate[l], w_exp_up[l], w_exp_down[l])
        h = _combine(dest_tiles, ys, rw.T, h1.reshape(n_tok, d), ada, g_post_ffn[l][None], seq)
        h = h.reshape(bsz, seq, d)
    return h
```

```python
import jax
import jax.numpy as jnp
from jax import lax
from jax.experimental import pallas as pl
from jax.experimental.pallas import tpu as pltpu

F32 = jnp.float32
BF16 = jnp.bfloat16

CHUNK = 64
HG_HEADS = 4
POOL_WINDOWS = (2, 4, 8, 16)
POOL_GROUPS = 4
MOE_GROUPS = 4
MOE_EPG = 8
N_EXPERTS = MOE_GROUPS * MOE_EPG
MOE_BLOCK = 256
EPS = 1e-6

MIX_ROWS = 256
MIX_SUBS = 2
POOL_HALO = 128
ROUTE_COLS = 512
MOVE_ROWS = 512
MOVE_UNROLL = 8
ROUTER_COLS = 128
LOG_DECAY_SPAN = 80.0
VMEM_LIMIT = 56 * 1024 * 1024

NT_DIMS = (((1,), (1,)), ((), ()))
TN_DIMS = (((0,), (0,)), ((), ()))


def _sigmoid(v):
    return 0.5 * jnp.tanh(0.5 * v) + 0.5


def _rms_norm(v, g):
    return v * lax.rsqrt(jnp.mean(v * v, axis=-1, keepdims=True) + EPS) * g


def _pack_bf16_pairs(v):
    n = v.shape[1] // 2
    lo = lax.bitcast_convert_type(v[:, :n].astype(BF16).astype(F32), jnp.uint32)
    hi = lax.bitcast_convert_type(v[:, n:].astype(BF16).astype(F32), jnp.uint32)
    return hi | (lo >> 16)


def _unpack_bf16_pairs(p):
    lo = lax.bitcast_convert_type(p << 16, F32)
    hi = lax.bitcast_convert_type(p & jnp.uint32(0xFFFF0000), F32)
    return jnp.concatenate([lo, hi], axis=1)


def _ada_kernel(c_ref, w_ref, b_ref, o_ref):
    c = c_ref[...]
    cond = c * _sigmoid(c)
    o_ref[...] = jnp.dot(cond.astype(BF16), w_ref[...].astype(BF16), preferred_element_type=F32) + b_ref[...]


def _ada(c, w_ada, b_ada):
    bsz, d = c.shape
    n_out = w_ada.shape[1]
    return pl.pallas_call(
        _ada_kernel,
        grid=(n_out // d,),
        in_specs=[pl.BlockSpec((bsz, d), lambda i: (0, 0)),
                  pl.BlockSpec((d, d), lambda i: (0, i)),
                  pl.BlockSpec((1, d), lambda i: (0, i))],
        out_specs=pl.BlockSpec((bsz, d), lambda i: (0, i)),
        out_shape=jax.ShapeDtypeStruct((bsz, n_out), F32),
        name="ada",
    )(c, w_ada, b_ada.reshape(1, n_out))


class _Sub:
    pass


def _mix_kernel(x_ref, ada_ref, gpre_ref, gpost_ref, gffn_ref, w1_ref, bgate_ref, lbl_ref, hgn_ref,
                tril_ref, band_ref, poolw_ref, pools_ref, wbh_ref, wbp_ref, wout_ref, wr_ref, br_ref,
                h1_ref, hn2_ref, eid_ref, rw_ref, cnt_ref,
                st_ref, ext_ref, o_scr):
    step_rows, d = x_ref.shape
    t_rows = MIX_ROWS
    hgw = hgn_ref.shape[1]
    hd = hgw // HG_HEADS
    pw = pools_ref.shape[1]
    gd = pw // POOL_GROUPS
    n_chunks = t_rows // CHUNK
    c_u = 4 * hgw
    c_gate = c_u + pw
    j = pl.program_id(1)

    @pl.when(j == 0)
    def _():
        st_ref[...] = jnp.zeros_like(st_ref)
        ext_ref[step_rows:step_rows + POOL_HALO, :] = jnp.zeros((POOL_HALO, pw), BF16)

    ext_ref[0:POOL_HALO, :] = ext_ref[step_rows:step_rows + POOL_HALO, :]

    ada = ada_ref[...]
    sh1, sc1, gt1 = ada[0:1], ada[1:2], ada[2:3]
    sh2, sc2 = ada[3:4], ada[4:5]
    lbl = lbl_ref[...]
    lmax = jnp.maximum(lbl[0:1], lbl[1:2])
    e0 = jnp.exp(lbl[0:1] - lmax)
    lb = e0 / (e0 + jnp.exp(lbl[1:2] - lmax))
    row = lax.broadcasted_iota(jnp.int32, (CHUNK, CHUNK), 0)
    col = lax.broadcasted_iota(jnp.int32, (CHUNK, CHUNK), 1)
    causal = row >= col

    def chunk_heads():
        for ci in range(n_chunks):
            for h in range(HG_HEADS):
                yield ci, h, slice(ci * CHUNK, (ci + 1) * CHUNK), slice(h * hd, (h + 1) * hd)

    def m_project(c):
        x = x_ref[c.rows, :]
        hn = _rms_norm(x, gpre_ref[...]) * (1.0 + sc1) + sh1
        c.proj = jnp.dot(hn.astype(BF16), w1_ref[...], preferred_element_type=F32)

    def e_gates(c):
        qr, fr = c.proj[:, 0:hgw], c.proj[:, hgw:2 * hgw]
        vr, gr = c.proj[:, 2 * hgw:3 * hgw], c.proj[:, 3 * hgw:4 * hgw]
        c.q = qr * _sigmoid(qr)
        f = lb + (1.0 - lb) * _sigmoid(fr)
        c.k = 1.0 - f
        lf = jnp.log(f)
        c.lf_hi = lf.astype(BF16)
        c.lf_lo = (lf - c.lf_hi.astype(F32)).astype(BF16)
        c.vb = vr.astype(BF16)
        c.og = hgn_ref[...] * (gr * _sigmoid(gr))
        c.u = c.proj[:, c_u:c_gate]
        ext_ref[c.ext0 + POOL_HALO:c.ext0 + POOL_HALO + t_rows, :] = c.u.astype(BF16)

    def m_cumsum(c):
        tril = tril_ref[...]
        c.b = (jnp.dot(tril, c.lf_hi, preferred_element_type=F32)
               + jnp.dot(tril, c.lf_lo, preferred_element_type=F32))

    def e_decays(c):
        c.qt, c.kt, c.kh, c.dec, c.mid = [], [], [], [], []
        for ci in range(n_chunks):
            rs = slice(ci * CHUNK, (ci + 1) * CHUNK)
            b_c = c.b[rs, :]
            b_mid = b_c[CHUNK // 2 - 1:CHUNK // 2, :]
            b_last = b_c[CHUNK - 1:CHUNK, :]
            rel = jnp.clip(b_c - b_mid, -LOG_DECAY_SPAN, LOG_DECAY_SPAN)
            c.qt.append((c.q[rs, :] * jnp.exp(rel)).astype(BF16))
            c.kt.append((c.k[rs, :] * jnp.exp(-rel)).astype(BF16))
            c.kh.append((c.k[rs, :] * jnp.exp(b_last - b_c)).astype(BF16))
            c.dec.append(jnp.exp(b_last))
            c.mid.append(jnp.exp(b_mid))

    def m_scores(c):
        c.s, c.upd = {}, {}
        for ci, h, rs, cs in chunk_heads():
            c.s[ci, h] = lax.dot_general(c.qt[ci][:, cs], c.kt[ci][:, cs], NT_DIMS, preferred_element_type=F32)
            c.upd[ci, h] = lax.dot_general(c.vb[rs, cs], c.kh[ci][:, cs], TN_DIMS,
                                           preferred_element_type=F32)

    def e_states(c):
        c.sb, c.stb = {}, {}
        for ci, h, rs, cs in chunk_heads():
            c.sb[ci, h] = jnp.where(causal, c.s[ci, h], 0.0).astype(BF16)
        for h in range(HG_HEADS):
            cs = slice(h * hd, (h + 1) * hd)
            st = st_ref[h]
            for ci in range(n_chunks):
                c.stb[ci, h] = (c.mid[ci][:, cs] * st).astype(BF16)
                st = c.dec[ci][:, cs] * st + c.upd[ci, h]
            st_ref[h] = st

    def m_outputs(c):
        c.o = {}
        for ci, h, rs, cs in chunk_heads():
            c.o[ci, h] = (jnp.dot(c.sb[ci, h], c.vb[rs, cs], preferred_element_type=F32)
                          + lax.dot_general(c.qt[ci][:, cs], c.stb[ci, h], NT_DIMS,
                                            preferred_element_type=F32))

    def e_head_norm(c):
        for ci, h, rs, cs in chunk_heads():
            o = c.o[ci, h]
            o = o * lax.rsqrt(jnp.mean(o * o, axis=-1, keepdims=True) + EPS)
            o_scr[c.row0 + ci * CHUNK:c.row0 + (ci + 1) * CHUNK, cs] = (o * c.og[rs, cs]).astype(BF16)

    def m_branch_hg(c):
        c.y_hg = jnp.dot(o_scr[c.rows, :], wbh_ref[...], preferred_element_type=F32)
        ext = ext_ref[c.ext0:c.ext0 + POOL_HALO + t_rows, :]
        c.wsum = [jnp.dot(band_ref[g], ext[:, g * gd:(g + 1) * gd], preferred_element_type=F32)
                  for g in range(POOL_GROUPS)]

    def e_pooled(c):
        pos = j * step_rows + c.row0 + lax.broadcasted_iota(jnp.int32, (t_rows, gd), 0)
        c.pooled = []
        for g in range(POOL_GROUPS):
            cnt = jnp.minimum(pos + 1, POOL_WINDOWS[g]).astype(F32)
            c.pooled.append((c.wsum[g] / cnt - c.u[:, g * gd:(g + 1) * gd]).astype(BF16))

    def m_pool_mix(c):
        c.mixed = [jnp.dot(c.pooled[g], poolw_ref[g], preferred_element_type=F32) for g in range(POOL_GROUPS)]

    def e_gate(c):
        c.mixed = (jnp.concatenate(c.mixed, axis=1) * pools_ref[...]).astype(BF16)
        c.gate = _sigmoid(c.proj[:, c_gate:] + bgate_ref[...])

    def m_branch_pool(c):
        c.y_pool = jnp.dot(c.mixed, wbp_ref[...], preferred_element_type=F32)

    def e_merge(c):
        c.merged = (c.gate[:, 0:d] * c.y_hg + c.gate[:, d:2 * d] * c.y_pool).astype(BF16)

    def m_out(c):
        c.mix = jnp.dot(c.merged, wout_ref[...], preferred_element_type=F32)

    def e_residual(c):
        h1 = x_ref[c.rows, :] + gt1 * _rms_norm(c.mix, gpost_ref[...])
        h1_ref[c.rows, :] = h1
        hn2 = _rms_norm(h1, gffn_ref[...]) * (1.0 + sc2) + sh2
        hn2_ref[c.rows, :] = _pack_bf16_pairs(hn2)
        hn2_hi = hn2.astype(BF16)
        c.hn2_split = jnp.concatenate([hn2_hi, (hn2 - hn2_hi.astype(F32)).astype(BF16)], axis=0)

    def m_router(c):
        c.prod = jnp.dot(c.hn2_split, wr_ref[...], preferred_element_type=F32)

    def e_route(c):
        n_r = wr_ref.shape[1] // 2
        logits = (c.prod[0:t_rows, 0:n_r] + c.prod[t_rows:2 * t_rows, 0:n_r]
                  + c.prod[0:t_rows, n_r:2 * n_r])
        lt = logits.T + br_ref[...]
        lg = lt[0:MOE_GROUPS]
        gmax = jnp.max(lg, axis=0, keepdims=True)
        p_g = 1.0 / jnp.sum(jnp.exp(lg - gmax), axis=0, keepdims=True)
        gi = lax.broadcasted_iota(jnp.int32, lg.shape, 0).astype(F32)
        g_idx = jnp.min(jnp.where(lg == gmax, gi, float(MOE_GROUPS)), axis=0, keepdims=True)
        le = lt[8:8 + MOE_EPG]
        for g in range(1, MOE_GROUPS):
            le = jnp.where(g_idx == float(g), lt[8 + g * MOE_EPG:8 + (g + 1) * MOE_EPG], le)
        ei = lax.broadcasted_iota(jnp.int32, le.shape, 0).astype(F32)
        m1 = jnp.max(le, axis=0, keepdims=True)
        i1 = jnp.min(jnp.where(le == m1, ei, float(MOE_EPG)), axis=0, keepdims=True)
        le2 = jnp.where(ei == i1, -jnp.inf, le)
        m2 = jnp.max(le2, axis=0, keepdims=True)
        i2 = jnp.min(jnp.where(le2 == m2, ei, float(MOE_EPG)), axis=0, keepdims=True)
        r = jnp.exp(m2 - m1)
        w_first = p_g / (1.0 + r)
        e_first, e_second = g_idx * MOE_EPG + i1, g_idx * MOE_EPG + i2
        eid_ref[0:1, c.rows] = e_first.astype(jnp.int32)
        eid_ref[1:2, c.rows] = e_second.astype(jnp.int32)
        ex = lax.broadcasted_iota(jnp.int32, (N_EXPERTS, t_rows), 0).astype(F32)
        hits = jnp.where((ex == e_first) | (ex == e_second), 1.0, 0.0)
        cnt = jnp.broadcast_to(jnp.sum(hits, axis=1, keepdims=True), cnt_ref.shape)
        cnt_ref[...] = cnt if c.row0 == 0 else cnt_ref[...] + cnt
        rw_ref[0:1, c.rows] = w_first
        rw_ref[1:2, c.rows] = w_first * r

    stages = [m_project, e_gates, m_cumsum, e_decays, m_scores, e_states, m_outputs, e_head_norm,
              m_branch_hg, e_pooled, m_pool_mix, e_gate, m_branch_pool, e_merge, m_out, e_residual,
              m_router, e_route]

    subs = []
    for k in range(step_rows // t_rows):
        c = _Sub()
        c.row0 = k * t_rows
        c.rows = slice(c.row0, c.row0 + t_rows)
        c.ext0 = k * t_rows
        subs.append(c)
    for t in range(len(stages) + len(subs) - 1):
        for k, c in enumerate(subs):
            if 0 <= t - k < len(stages):
                stages[t - k](c)


def _band_matrices(t_rows):
    t = jnp.arange(t_rows)[:, None] + POOL_HALO
    jx = jnp.arange(t_rows + POOL_HALO)[None, :]
    return jnp.stack([((jx <= t) & (jx > t - w)) for w in POOL_WINDOWS]).astype(BF16)


def _chunk_tril(t_rows):
    r = jnp.arange(t_rows)[:, None]
    c = jnp.arange(t_rows)[None, :]
    return ((r >= c) & (r // CHUNK == c // CHUNK)).astype(BF16)


def _mixer(x, ada, g_pre, g_post, g_ffn, w1, b_gate, lb_logits, hg_norm_g, pool_w, pool_scale,
           w_bh, w_bp, w_out, w_router_t, b_router):
    bsz, seq, d = x.shape
    t_rows = MIX_ROWS
    step_rows = MIX_ROWS * MIX_SUBS
    n_t = seq // step_rows
    n_tok = bsz * seq
    hgw = hg_norm_g.shape[1]
    pw = pool_scale.shape[1]
    n_r = b_router.shape[0]

    def const(shape):
        return pl.BlockSpec(shape, lambda b, j: (0,) * len(shape), pipeline_mode=pl.Buffered(1))

    in_specs = [
        pl.BlockSpec((None, step_rows, d), lambda b, j: (b, j, 0)),
        pl.BlockSpec((None, 6, d), lambda b, j: (b, 0, 0)),
        const((1, d)), const((1, d)), const((1, d)),
        const(w1.shape), const((1, 2 * d)),
        const(lb_logits.shape), const((1, hgw)),
        const((t_rows, t_rows)), const((POOL_GROUPS, t_rows, t_rows + POOL_HALO)),
        const(pool_w.shape), const((1, pw)),
        const(w_bh.shape), const(w_bp.shape), const(w_out.shape),
        const(w_router_t.shape), const((n_r, 1)),
    ]
    out_specs = [
        pl.BlockSpec((None, step_rows, d), lambda b, j: (b, j, 0)),
        pl.BlockSpec((None, step_rows, d // 2), lambda b, j: (b, j, 0)),
        pl.BlockSpec((2, step_rows), lambda b, j: (0, b * n_t + j)),
        pl.BlockSpec((2, step_rows), lambda b, j: (0, b * n_t + j)),
        pl.BlockSpec((None, N_EXPERTS, 128), lambda b, j: (b * n_t + j, 0, 0)),
    ]
    out_shape = [
        jax.ShapeDtypeStruct((bsz, seq, d), F32),
        jax.ShapeDtypeStruct((bsz, seq, d // 2), jnp.uint32),
        jax.ShapeDtypeStruct((2, n_tok), jnp.int32),
        jax.ShapeDtypeStruct((2, n_tok), F32),
        jax.ShapeDtypeStruct((bsz * n_t, N_EXPERTS, 128), F32),
    ]
    scratch = [
        pltpu.VMEM((HG_HEADS, hgw // HG_HEADS, hgw // HG_HEADS), F32),
        pltpu.VMEM((step_rows + POOL_HALO, pw), BF16),
        pltpu.VMEM((step_rows, hgw), BF16),
    ]
    return pl.pallas_call(
        _mix_kernel,
        grid=(bsz, n_t),
        in_specs=in_specs, out_specs=out_specs, out_shape=out_shape, scratch_shapes=scratch,
        compiler_params=pltpu.CompilerParams(dimension_semantics=("arbitrary", "arbitrary"),
                                             vmem_limit_bytes=VMEM_LIMIT),
        name="mixer",
    )(x, ada, g_pre, g_post, g_ffn, w1, b_gate, lb_logits, hg_norm_g, _chunk_tril(t_rows),
      _band_matrices(t_rows), pool_w, pool_scale, w_bh, w_bp, w_out, w_router_t, b_router)


def _route_kernel(eid_ref, cnt_ref, dest_ref, blk_ref, meta_ref, run_ref):
    i = pl.program_id(0)
    cols = eid_ref.shape[1]
    eidx = lax.broadcasted_iota(jnp.int32, (N_EXPERTS, cols), 0)
    hot0 = eidx == eid_ref[0:1, :]
    hot1 = eidx == eid_ref[1:2, :]
    both = jnp.where(hot0 | hot1, 1.0, 0.0)
    tile_cnt = jnp.sum(both, axis=1, keepdims=True)

    @pl.when(i == 0)
    def _():
        total = jnp.sum(cnt_ref[...], axis=0)
        nblk_f = jnp.floor((total + float(MOE_BLOCK - 1)) * (1.0 / MOE_BLOCK))
        er = lax.broadcasted_iota(jnp.int32, (N_EXPERTS, N_EXPERTS), 0)
        ec = lax.broadcasted_iota(jnp.int32, (N_EXPERTS, N_EXPERTS), 1)
        lower = jnp.where(ec < er, 1.0, 0.0).astype(BF16)
        start_blk = jnp.dot(lower, nblk_f.astype(BF16), preferred_element_type=F32)
        run_ref[...] = start_blk[:, 0:1] * float(MOE_BLOCK)
        end_blk = start_blk + nblk_f
        lane = lax.broadcasted_iota(jnp.int32, (N_EXPERTS, blk_ref.shape[1]), 1).astype(F32)
        done = jnp.where(end_blk[:, 0:1] <= lane, 1.0, 0.0)
        blk_ref[...] = jnp.minimum(jnp.sum(done, axis=0, keepdims=True),
                                   float(N_EXPERTS - 1)).astype(jnp.int32)
        meta_ref[...] = jnp.broadcast_to(end_blk[N_EXPERTS - 1:N_EXPERTS, 0:1],
                                         meta_ref.shape).astype(jnp.int32)

    r = lax.broadcasted_iota(jnp.int32, (cols, cols), 0)
    c = lax.broadcasted_iota(jnp.int32, (cols, cols), 1)
    before = jnp.where(r < c, 1.0, 0.0).astype(BF16)
    prefix = jnp.dot(both.astype(BF16), before, preferred_element_type=F32)
    slot = run_ref[...] + prefix
    dest_ref[0:1, :] = jnp.sum(jnp.where(hot0, slot, 0.0), axis=0, keepdims=True).astype(jnp.int32)
    dest_ref[1:2, :] = jnp.sum(jnp.where(hot1, slot, 0.0), axis=0, keepdims=True).astype(jnp.int32)
    run_ref[...] += tile_cnt


def _route(eid, step_counts, n_blocks):
    n_tok = eid.shape[1]
    cols = ROUTE_COLS
    blk_lanes = pl.cdiv(n_blocks, 128) * 128
    return pl.pallas_call(
        _route_kernel,
        grid=(n_tok // cols,),
        in_specs=[pl.BlockSpec((2, cols), lambda i: (0, i)),
                  pl.BlockSpec(step_counts.shape, lambda i: (0, 0, 0))],
        out_specs=[pl.BlockSpec((2, cols), lambda i: (0, i)),
                   pl.BlockSpec((1, blk_lanes), lambda i: (0, 0)),
                   pl.BlockSpec((1, 128), lambda i: (0, 0))],
        out_shape=[jax.ShapeDtypeStruct((2, n_tok), jnp.int32),
                   jax.ShapeDtypeStruct((1, blk_lanes), jnp.int32),
                   jax.ShapeDtypeStruct((1, 128), jnp.int32)],
        scratch_shapes=[pltpu.VMEM((N_EXPERTS, 1), F32)],
        compiler_params=pltpu.CompilerParams(dimension_semantics=("arbitrary",)),
        name="route",
    )(eid, step_counts)


def _row_copy(src_ref, src_row, dst_ref, dst_row, sem):
    return pltpu.make_async_copy(src_ref.at[pl.ds(src_row, 1)], dst_ref.at[pl.ds(dst_row, 1)], sem)


def _drain_rows(rows, wait_pair):
    def drain(g, carry):
        for _ in range(MOVE_UNROLL):
            wait_pair()
        return carry

    lax.fori_loop(0, rows // MOVE_UNROLL, drain, 0)


def _dispatch_kernel(dest_ref, hn2_ref, xs_in_ref, xs_ref, sem):
    del xs_in_ref
    rows = hn2_ref.shape[0]

    def issue(g, carry):
        base = pl.multiple_of(g * MOVE_UNROLL, MOVE_UNROLL)
        group = hn2_ref.at[pl.ds(base, MOVE_UNROLL)]
        for u in range(MOVE_UNROLL):
            _row_copy(group, u, xs_ref, dest_ref[base + u], sem).start(priority=0)
            _row_copy(group, u, xs_ref, dest_ref[rows + base + u], sem).start(priority=1)
        return carry

    lax.fori_loop(0, rows // MOVE_UNROLL, issue, 0)

    def wait_pair():
        _row_copy(hn2_ref, 0, xs_ref, 0, sem).wait()
        _row_copy(hn2_ref, 0, xs_ref, 0, sem).wait()

    _drain_rows(rows, wait_pair)


def _dispatch(dest_tiles, hn2, n_rows_out):
    n_tok, d = hn2.shape
    rows = MOVE_ROWS
    xs0 = jnp.zeros((n_rows_out, d), hn2.dtype)
    return pl.pallas_call(
        _dispatch_kernel,
        grid=(n_tok // rows,),
        in_specs=[pl.BlockSpec((2 * rows,), lambda i: (i,), memory_space=pltpu.SMEM),
                  pl.BlockSpec((rows, d), lambda i: (i, 0)),
                  pl.BlockSpec(memory_space=pl.ANY)],
        out_specs=pl.BlockSpec(memory_space=pl.ANY),
        out_shape=jax.ShapeDtypeStruct((n_rows_out, d), hn2.dtype),
        scratch_shapes=[pltpu.SemaphoreType.DMA],
        input_output_aliases={2: 0},
        compiler_params=pltpu.CompilerParams(dimension_semantics=("arbitrary",)),
        name="dispatch",
    )(dest_tiles, hn2, xs0)


def _expert_kernel(blk_ref, meta_ref, xs_ref, wg_ref, wu_ref, wd_ref, ys_ref, wg_s, wu_s, wd_s):
    j = pl.program_id(0)
    used = j < meta_ref[0]
    first_of_expert = (j == 0) | (blk_ref[j] != blk_ref[jnp.maximum(j - 1, 0)])

    @pl.when(used & first_of_expert)
    def _():
        wg_s[...] = wg_ref[...].astype(BF16)
        wu_s[...] = wu_ref[...].astype(BF16)
        wd_s[...] = wd_ref[...].astype(BF16)

    @pl.when(used)
    def _():
        xb = _unpack_bf16_pairs(xs_ref[...]).astype(BF16)
        gp = jnp.dot(xb, wg_s[...], preferred_element_type=F32)
        up = jnp.dot(xb, wu_s[...], preferred_element_type=F32)
        act = gp * _sigmoid(gp) * up
        ys_ref[...] = _pack_bf16_pairs(jnp.dot(act.astype(BF16), wd_s[...], preferred_element_type=F32))

    @pl.when(jnp.logical_not(used))
    def _():
        ys_ref[...] = jnp.zeros_like(ys_ref)


def _experts(block_e, n_used, xs, w_gate, w_up, w_down, n_blocks):
    dp = xs.shape[1]
    ff, d = w_down.shape[1], w_down.shape[2]

    def row_block(j, be, nu):
        return (jnp.minimum(j, nu[0] - 1), 0)

    def expert_block(j, be, nu):
        return (be[jnp.minimum(j, nu[0] - 1)], 0, 0)

    grid_spec = pltpu.PrefetchScalarGridSpec(
        num_scalar_prefetch=2,
        grid=(n_blocks,),
        in_specs=[pl.BlockSpec((MOE_BLOCK, dp), row_block),
                  pl.BlockSpec((None, d, ff), expert_block),
                  pl.BlockSpec((None, d, ff), expert_block),
                  pl.BlockSpec((None, ff, d), expert_block)],
        out_specs=pl.BlockSpec((MOE_BLOCK, dp), lambda j, be, nu: (j, 0)),
        scratch_shapes=[pltpu.VMEM((d, ff), BF16), pltpu.VMEM((d, ff), BF16), pltpu.VMEM((ff, d), BF16)],
    )
    return pl.pallas_call(
        _expert_kernel,
        grid_spec=grid_spec,
        out_shape=jax.ShapeDtypeStruct(xs.shape, xs.dtype),
        compiler_params=pltpu.CompilerParams(dimension_semantics=("arbitrary",),
                                             vmem_limit_bytes=VMEM_LIMIT),
        name="experts",
    )(block_e, n_used, xs, w_gate, w_up, w_down)


def _combine_kernel(dest_ref, ys_ref, rw_ref, h1_ref, ada_ref, g_ref, out_ref, buf0, buf1, sem):
    rows = h1_ref.shape[0]

    def issue(g, carry):
        base = pl.multiple_of(g * MOVE_UNROLL, MOVE_UNROLL)
        group0 = buf0.at[pl.ds(base, MOVE_UNROLL)]
        group1 = buf1.at[pl.ds(base, MOVE_UNROLL)]
        for u in range(MOVE_UNROLL):
            _row_copy(ys_ref, dest_ref[base + u], group0, u, sem).start(priority=0)
            _row_copy(ys_ref, dest_ref[rows + base + u], group1, u, sem).start(priority=1)
        return carry

    lax.fori_loop(0, rows // MOVE_UNROLL, issue, 0)

    def wait_pair():
        _row_copy(ys_ref, 0, buf0, 0, sem).wait()
        _row_copy(ys_ref, 0, buf1, 0, sem).wait()

    _drain_rows(rows, wait_pair)

    rw = rw_ref[...]
    y = rw[:, 0:1] * _unpack_bf16_pairs(buf0[...]) + rw[:, 1:2] * _unpack_bf16_pairs(buf1[...])
    gt2 = ada_ref[5:6, :]
    out_ref[...] = h1_ref[...] + gt2 * _rms_norm(y, g_ref[...])


def _combine(dest_tiles, ys, rw_cols, h1, ada, g_post, seq):
    n_tok, d = h1.shape
    rows = MOVE_ROWS
    per_batch = seq // rows
    return pl.pallas_call(
        _combine_kernel,
        grid=(n_tok // rows,),
        in_specs=[pl.BlockSpec((2 * rows,), lambda i: (i,), memory_space=pltpu.SMEM),
                  pl.BlockSpec(memory_space=pl.ANY),
                  pl.BlockSpec((rows, 2), lambda i: (i, 0)),
                  pl.BlockSpec((rows, d), lambda i: (i, 0)),
                  pl.BlockSpec((None, 6, d), lambda i: (i // per_batch, 0, 0)),
                  pl.BlockSpec((1, d), lambda i: (0, 0))],
        out_specs=pl.BlockSpec((rows, d), lambda i: (i, 0)),
        out_shape=jax.ShapeDtypeStruct((n_tok, d), F32),
        scratch_shapes=[pltpu.VMEM((rows, ys.shape[1]), ys.dtype), pltpu.VMEM((rows, ys.shape[1]), ys.dtype),
                        pltpu.SemaphoreType.DMA],
        compiler_params=pltpu.CompilerParams(dimension_semantics=("arbitrary",)),
        name="combine",
    )(dest_tiles, ys, rw_cols, h1, ada, g_post)


def kernel(x, c, w_ada, b_ada, g_pre_mix, g_post_mix, w_in, hg_lb_logits, hg_norm_g, pool_w, pool_scale,
           w_branch_hg, w_branch_pool, w_gate, b_gate, w_out, g_pre_ffn, g_post_ffn, w_router_group,
           b_router_group, w_router_expert, b_router_expert, w_exp_gate, w_exp_up, w_exp_down):
    depth = w_in.shape[0]
    bsz, seq, d = x.shape
    n_tok = bsz * seq
    n_blocks = -(-(n_tok * 2) // MOE_BLOCK) + N_EXPERTS
    assert seq % (MIX_ROWS * MIX_SUBS) == 0 and MIX_ROWS % CHUNK == 0 and seq % MOVE_ROWS == 0
    assert n_tok % ROUTE_COLS == 0 and hg_lb_logits.shape[0] == 2 and depth == 1

    h = x
    for l in range(depth):
        ada = _ada(c, w_ada[l], b_ada[l]).reshape(bsz, 6, d)
        w1 = jnp.concatenate([w_in[l], w_gate[l]], axis=1).astype(BF16)
        pad_g = jnp.zeros((d, 8 - MOE_GROUPS), F32)
        pad_e = jnp.zeros((d, ROUTER_COLS - 8 - N_EXPERTS), F32)
        w_router = jnp.concatenate([w_router_group[l], pad_g, w_router_expert[l], pad_e], axis=1)
        w_router_hi = w_router.astype(BF16)
        w_router_lo = (w_router - w_router_hi.astype(F32)).astype(BF16)
        w_router_t = jnp.concatenate([w_router_hi, w_router_lo], axis=1)
        b_router = jnp.concatenate(
            [b_router_group[l], jnp.zeros((8 - MOE_GROUPS,), F32), b_router_expert[l],
             jnp.zeros((ROUTER_COLS - 8 - N_EXPERTS,), F32)])[:, None]
        h1, hn2, eid, rw, step_counts = _mixer(
            h, ada, g_pre_mix[l][None], g_post_mix[l][None], g_pre_ffn[l][None], w1, b_gate[l][None],
            hg_lb_logits, hg_norm_g[l][None], pool_w[l].astype(BF16), pool_scale[l][None],
            w_branch_hg[l].astype(BF16), w_branch_pool[l].astype(BF16), w_out[l].astype(BF16),
            w_router_t, b_router)

        dest, block_e, meta = _route(eid, step_counts, n_blocks)
        n_mt = n_tok // MOVE_ROWS
        dest_tiles = dest.reshape(2, n_mt, MOVE_ROWS).transpose(1, 0, 2).reshape(-1)
        xs = _dispatch(dest_tiles, hn2.reshape(n_tok, d // 2), n_blocks * MOE_BLOCK)
        ys = _experts(block_e[0, :n_blocks], meta[0, :1], xs, w_exp_gate[l], w_exp_up[l], w_exp_down[l],
                      n_blocks)
        h = _combine(dest_tiles, ys, rw.T, h1.reshape(n_tok, d), ada, g_post_ffn[l][None], seq)
        h = h.reshape(bsz, seq, d)
    return h
```

```python
import jax
import jax.numpy as jnp
from jax import lax
from jax.experimental import pallas as pl
from jax.experimental.pallas import tpu as pltpu

F32 = jnp.float32
BF16 = jnp.bfloat16

CHUNK = 64
HG_HEADS = 4
POOL_WINDOWS = (2, 4, 8, 16)
POOL_GROUPS = 4
MOE_GROUPS = 4
MOE_EPG = 8
N_EXPERTS = MOE_GROUPS * MOE_EPG
MOE_BLOCK = 256
EPS = 1e-6

MIX_ROWS = 256
MIX_SUBS = 2
POOL_HALO = 128
ROUTE_COLS = 512
MOVE_ROWS = 1024
MOVE_UNROLL = 8
ROUTER_COLS = 128
LOG_DECAY_SPAN = 80.0
VMEM_LIMIT = 56 * 1024 * 1024

NT_DIMS = (((1,), (1,)), ((), ()))
TN_DIMS = (((0,), (0,)), ((), ()))


def _sigmoid(v):
    return 0.5 * jnp.tanh(0.5 * v) + 0.5


def _rms_norm(v, g):
    return v * lax.rsqrt(jnp.mean(v * v, axis=-1, keepdims=True) + EPS) * g


def _pack_bf16_pairs(v):
    n = v.shape[1] // 2
    lo = lax.bitcast_convert_type(v[:, :n].astype(BF16).astype(F32), jnp.uint32)
    hi = lax.bitcast_convert_type(v[:, n:].astype(BF16).astype(F32), jnp.uint32)
    return hi | (lo >> 16)


def _unpack_bf16_pairs(p):
    lo = lax.bitcast_convert_type(p << 16, F32)
    hi = lax.bitcast_convert_type(p & jnp.uint32(0xFFFF0000), F32)
    return jnp.concatenate([lo, hi], axis=1)


def _ada_kernel(c_ref, w_ref, b_ref, o_ref):
    c = c_ref[...]
    cond = c * _sigmoid(c)
    o_ref[...] = jnp.dot(cond.astype(BF16), w_ref[...].astype(BF16), preferred_element_type=F32) + b_ref[...]


def _ada(c, w_ada, b_ada):
    bsz, d = c.shape
    n_out = w_ada.shape[1]
    return pl.pallas_call(
        _ada_kernel,
        grid=(n_out // d,),
        in_specs=[pl.BlockSpec((bsz, d), lambda i: (0, 0)),
                  pl.BlockSpec((d, d), lambda i: (0, i)),
                  pl.BlockSpec((1, d), lambda i: (0, i))],
        out_specs=pl.BlockSpec((bsz, d), lambda i: (0, i)),
        out_shape=jax.ShapeDtypeStruct((bsz, n_out), F32),
        name="ada",
    )(c, w_ada, b_ada.reshape(1, n_out))


class _Sub:
    pass


def _mix_kernel(x_ref, ada_ref, gpre_ref, gpost_ref, gffn_ref, w1_ref, bgate_ref, lbl_ref, hgn_ref,
                tril_ref, band_ref, poolw_ref, pools_ref, wbh_ref, wbp_ref, wout_ref, wr_ref, br_ref,
                h1_ref, hn2_ref, eid_ref, rw_ref, cnt_ref, blank_ref,
                st_ref, ext_ref, o_scr):
    step_rows, d = x_ref.shape
    t_rows = MIX_ROWS
    hgw = hgn_ref.shape[1]
    hd = hgw // HG_HEADS
    pw = pools_ref.shape[1]
    gd = pw // POOL_GROUPS
    n_chunks = t_rows // CHUNK
    c_u = 4 * hgw
    c_gate = c_u + pw
    j = pl.program_id(1)

    @pl.when(j == 0)
    def _():
        st_ref[...] = jnp.zeros_like(st_ref)
        ext_ref[step_rows:step_rows + POOL_HALO, :] = jnp.zeros((POOL_HALO, pw), BF16)

    ext_ref[0:POOL_HALO, :] = ext_ref[step_rows:step_rows + POOL_HALO, :]

    blank_ref[...] = jnp.zeros(blank_ref.shape, blank_ref.dtype)

    ada = ada_ref[...]
    sh1, sc1, gt1 = ada[0:1], ada[1:2], ada[2:3]
    sh2, sc2 = ada[3:4], ada[4:5]
    lbl = lbl_ref[...]
    lmax = jnp.maximum(lbl[0:1], lbl[1:2])
    e0 = jnp.exp(lbl[0:1] - lmax)
    lb = e0 / (e0 + jnp.exp(lbl[1:2] - lmax))
    row = lax.broadcasted_iota(jnp.int32, (CHUNK, CHUNK), 0)
    col = lax.broadcasted_iota(jnp.int32, (CHUNK, CHUNK), 1)
    causal = row >= col

    def chunk_heads():
        for ci in range(n_chunks):
            for h in range(HG_HEADS):
                yield ci, h, slice(ci * CHUNK, (ci + 1) * CHUNK), slice(h * hd, (h + 1) * hd)

    def m_project(c):
        x = x_ref[c.rows, :]
        hn = _rms_norm(x, gpre_ref[...]) * (1.0 + sc1) + sh1
        c.proj = jnp.dot(hn.astype(BF16), w1_ref[...], preferred_element_type=F32)

    def e_gates(c):
        qr, fr = c.proj[:, 0:hgw], c.proj[:, hgw:2 * hgw]
        vr, gr = c.proj[:, 2 * hgw:3 * hgw], c.proj[:, 3 * hgw:4 * hgw]
        c.q = qr * _sigmoid(qr)
        f = lb + (1.0 - lb) * _sigmoid(fr)
        c.k = 1.0 - f
        lf = jnp.log(f)
        c.lf_hi = lf.astype(BF16)
        c.lf_lo = (lf - c.lf_hi.astype(F32)).astype(BF16)
        c.vb = vr.astype(BF16)
        c.og = hgn_ref[...] * (gr * _sigmoid(gr))
        c.u = c.proj[:, c_u:c_gate]
        ext_ref[c.ext0 + POOL_HALO:c.ext0 + POOL_HALO + t_rows, :] = c.u.astype(BF16)

    def m_cumsum(c):
        tril = tril_ref[...]
        c.b = (jnp.dot(tril, c.lf_hi, preferred_element_type=F32)
               + jnp.dot(tril, c.lf_lo, preferred_element_type=F32))

    def e_decays(c):
        c.qt, c.kt, c.kh, c.dec, c.mid = [], [], [], [], []
        for ci in range(n_chunks):
            rs = slice(ci * CHUNK, (ci + 1) * CHUNK)
            b_c = c.b[rs, :]
            b_mid = b_c[CHUNK // 2 - 1:CHUNK // 2, :]
            b_last = b_c[CHUNK - 1:CHUNK, :]
            rel = jnp.clip(b_c - b_mid, -LOG_DECAY_SPAN, LOG_DECAY_SPAN)
            c.qt.append((c.q[rs, :] * jnp.exp(rel)).astype(BF16))
            c.kt.append((c.k[rs, :] * jnp.exp(-rel)).astype(BF16))
            c.kh.append((c.k[rs, :] * jnp.exp(b_last - b_c)).astype(BF16))
            c.dec.append(jnp.exp(b_last))
            c.mid.append(jnp.exp(b_mid))

    def m_scores(c):
        c.s, c.upd = {}, {}
        for ci, h, rs, cs in chunk_heads():
            c.s[ci, h] = lax.dot_general(c.qt[ci][:, cs], c.kt[ci][:, cs], NT_DIMS, preferred_element_type=F32)
            c.upd[ci, h] = lax.dot_general(c.vb[rs, cs], c.kh[ci][:, cs], TN_DIMS,
                                           preferred_element_type=F32)

    def e_states(c):
        c.sb, c.stb = {}, {}
        for ci, h, rs, cs in chunk_heads():
            c.sb[ci, h] = jnp.where(causal, c.s[ci, h], 0.0).astype(BF16)
        for h in range(HG_HEADS):
            cs = slice(h * hd, (h + 1) * hd)
            st = st_ref[h]
            for ci in range(n_chunks):
                c.stb[ci, h] = (c.mid[ci][:, cs] * st).astype(BF16)
                st = c.dec[ci][:, cs] * st + c.upd[ci, h]
            st_ref[h] = st

    def m_outputs(c):
        c.o = {}
        for ci, h, rs, cs in chunk_heads():
            c.o[ci, h] = (jnp.dot(c.sb[ci, h], c.vb[rs, cs], preferred_element_type=F32)
                          + lax.dot_general(c.qt[ci][:, cs], c.stb[ci, h], NT_DIMS,
                                            preferred_element_type=F32))

    def e_head_norm(c):
        for ci, h, rs, cs in chunk_heads():
            o = c.o[ci, h]
            o = o * lax.rsqrt(jnp.mean(o * o, axis=-1, keepdims=True) + EPS)
            o_scr[c.row0 + ci * CHUNK:c.row0 + (ci + 1) * CHUNK, cs] = (o * c.og[rs, cs]).astype(BF16)

    def m_branch_hg(c):
        c.y_hg = jnp.dot(o_scr[c.rows, :], wbh_ref[...], preferred_element_type=F32)
        ext = ext_ref[c.ext0:c.ext0 + POOL_HALO + t_rows, :]
        c.wsum = [jnp.dot(band_ref[g], ext[:, g * gd:(g + 1) * gd], preferred_element_type=F32)
                  for g in range(POOL_GROUPS)]

    def e_pooled(c):
        pos = j * step_rows + c.row0 + lax.broadcasted_iota(jnp.int32, (t_rows, gd), 0)
        c.pooled = []
        for g in range(POOL_GROUPS):
            cnt = jnp.minimum(pos + 1, POOL_WINDOWS[g]).astype(F32)
            c.pooled.append((c.wsum[g] / cnt - c.u[:, g * gd:(g + 1) * gd]).astype(BF16))

    def m_pool_mix(c):
        c.mixed = [jnp.dot(c.pooled[g], poolw_ref[g], preferred_element_type=F32) for g in range(POOL_GROUPS)]

    def e_gate(c):
        c.mixed = (jnp.concatenate(c.mixed, axis=1) * pools_ref[...]).astype(BF16)
        c.gate = _sigmoid(c.proj[:, c_gate:] + bgate_ref[...])

    def m_branch_pool(c):
        c.y_pool = jnp.dot(c.mixed, wbp_ref[...], preferred_element_type=F32)

    def e_merge(c):
        c.merged = (c.gate[:, 0:d] * c.y_hg + c.gate[:, d:2 * d] * c.y_pool).astype(BF16)

    def m_out(c):
        c.mix = jnp.dot(c.merged, wout_ref[...], preferred_element_type=F32)

    def e_residual(c):
        h1 = x_ref[c.rows, :] + gt1 * _rms_norm(c.mix, gpost_ref[...])
        h1_ref[c.rows, :] = h1
        hn2 = _rms_norm(h1, gffn_ref[...]) * (1.0 + sc2) + sh2
        hn2_ref[c.rows, :] = _pack_bf16_pairs(hn2)
        hn2_hi = hn2.astype(BF16)
        c.hn2_split = jnp.concatenate([hn2_hi, (hn2 - hn2_hi.astype(F32)).astype(BF16)], axis=0)

    def m_router(c):
        c.prod = jnp.dot(c.hn2_split, wr_ref[...], preferred_element_type=F32)

    def e_route(c):
        n_r = wr_ref.shape[1] // 2
        logits = (c.prod[0:t_rows, 0:n_r] + c.prod[t_rows:2 * t_rows, 0:n_r]
                  + c.prod[0:t_rows, n_r:2 * n_r])
        lt = logits.T + br_ref[...]
        lg = lt[0:MOE_GROUPS]
        gmax = jnp.max(lg, axis=0, keepdims=True)
        p_g = 1.0 / jnp.sum(jnp.exp(lg - gmax), axis=0, keepdims=True)
        gi = lax.broadcasted_iota(jnp.int32, lg.shape, 0).astype(F32)
        g_idx = jnp.min(jnp.where(lg == gmax, gi, float(MOE_GROUPS)), axis=0, keepdims=True)
        le = lt[8:8 + MOE_EPG]
        for g in range(1, MOE_GROUPS):
            le = jnp.where(g_idx == float(g), lt[8 + g * MOE_EPG:8 + (g + 1) * MOE_EPG], le)
        ei = lax.broadcasted_iota(jnp.int32, le.shape, 0).astype(F32)
        m1 = jnp.max(le, axis=0, keepdims=True)
        i1 = jnp.min(jnp.where(le == m1, ei, float(MOE_EPG)), axis=0, keepdims=True)
        le2 = jnp.where(ei == i1, -jnp.inf, le)
        m2 = jnp.max(le2, axis=0, keepdims=True)
        i2 = jnp.min(jnp.where(le2 == m2, ei, float(MOE_EPG)), axis=0, keepdims=True)
        r = jnp.exp(m2 - m1)
        w_first = p_g / (1.0 + r)
        e_first, e_second = g_idx * MOE_EPG + i1, g_idx * MOE_EPG + i2
        eid_ref[0:1, c.rows] = e_first.astype(jnp.int32)
        eid_ref[1:2, c.rows] = e_second.astype(jnp.int32)
        ex = lax.broadcasted_iota(jnp.int32, (N_EXPERTS, t_rows), 0).astype(F32)
        hits = jnp.where((ex == e_first) | (ex == e_second), 1.0, 0.0)
        cnt = jnp.broadcast_to(jnp.sum(hits, axis=1, keepdims=True), cnt_ref.shape)
        cnt_ref[...] = cnt if c.row0 == 0 else cnt_ref[...] + cnt
        rw_ref[0:1, c.rows] = w_first
        rw_ref[1:2, c.rows] = w_first * r

    stages = [m_project, e_gates, m_cumsum, e_decays, m_scores, e_states, m_outputs, e_head_norm,
              m_branch_hg, e_pooled, m_pool_mix, e_gate, m_branch_pool, e_merge, m_out, e_residual,
              m_router, e_route]

    subs = []
    for k in range(step_rows // t_rows):
        c = _Sub()
        c.row0 = k * t_rows
        c.rows = slice(c.row0, c.row0 + t_rows)
        c.ext0 = k * t_rows
        subs.append(c)
    for t in range(len(stages) + len(subs) - 1):
        for k, c in enumerate(subs):
            if 0 <= t - k < len(stages):
                stages[t - k](c)


def _band_matrices(t_rows):
    t = jnp.arange(t_rows)[:, None] + POOL_HALO
    jx = jnp.arange(t_rows + POOL_HALO)[None, :]
    return jnp.stack([((jx <= t) & (jx > t - w)) for w in POOL_WINDOWS]).astype(BF16)


def _chunk_tril(t_rows):
    r = jnp.arange(t_rows)[:, None]
    c = jnp.arange(t_rows)[None, :]
    return ((r >= c) & (r // CHUNK == c // CHUNK)).astype(BF16)


def _mixer(x, ada, g_pre, g_post, g_ffn, w1, b_gate, lb_logits, hg_norm_g, pool_w, pool_scale,
           w_bh, w_bp, w_out, w_router_t, b_router, blank_rows):
    bsz, seq, d = x.shape
    t_rows = MIX_ROWS
    step_rows = MIX_ROWS * MIX_SUBS
    n_t = seq // step_rows
    n_tok = bsz * seq
    hgw = hg_norm_g.shape[1]
    pw = pool_scale.shape[1]
    n_r = b_router.shape[0]

    def const(shape):
        return pl.BlockSpec(shape, lambda b, j: (0,) * len(shape), pipeline_mode=pl.Buffered(1))

    in_specs = [
        pl.BlockSpec((None, step_rows, d), lambda b, j: (b, j, 0)),
        pl.BlockSpec((None, 6, d), lambda b, j: (b, 0, 0)),
        const((1, d)), const((1, d)), const((1, d)),
        const(w1.shape), const((1, 2 * d)),
        const(lb_logits.shape), const((1, hgw)),
        const((t_rows, t_rows)), const((POOL_GROUPS, t_rows, t_rows + POOL_HALO)),
        const(pool_w.shape), const((1, pw)),
        const(w_bh.shape), const(w_bp.shape), const(w_out.shape),
        const(w_router_t.shape), const((n_r, 1)),
    ]
    out_specs = [
        pl.BlockSpec((None, step_rows, d), lambda b, j: (b, j, 0)),
        pl.BlockSpec((None, step_rows, d // 2), lambda b, j: (b, j, 0)),
        pl.BlockSpec((2, step_rows), lambda b, j: (0, b * n_t + j)),
        pl.BlockSpec((2, step_rows), lambda b, j: (0, b * n_t + j)),
        pl.BlockSpec((None, N_EXPERTS, 128), lambda b, j: (b * n_t + j, 0, 0)),
        pl.BlockSpec((blank_rows // (bsz * n_t), d // 2), lambda b, j: (b * n_t + j, 0)),
    ]
    out_shape = [
        jax.ShapeDtypeStruct((bsz, seq, d), F32),
        jax.ShapeDtypeStruct((bsz, seq, d // 2), jnp.uint32),
        jax.ShapeDtypeStruct((2, n_tok), jnp.int32),
        jax.ShapeDtypeStruct((2, n_tok), F32),
        jax.ShapeDtypeStruct((bsz * n_t, N_EXPERTS, 128), F32),
        jax.ShapeDtypeStruct((blank_rows, d // 2), jnp.uint32),
    ]
    assert blank_rows % (bsz * n_t * 8) == 0
    scratch = [
        pltpu.VMEM((HG_HEADS, hgw // HG_HEADS, hgw // HG_HEADS), F32),
        pltpu.VMEM((step_rows + POOL_HALO, pw), BF16),
        pltpu.VMEM((step_rows, hgw), BF16),
    ]
    return pl.pallas_call(
        _mix_kernel,
        grid=(bsz, n_t),
        in_specs=in_specs, out_specs=out_specs, out_shape=out_shape, scratch_shapes=scratch,
        compiler_params=pltpu.CompilerParams(dimension_semantics=("arbitrary", "arbitrary"),
                                             vmem_limit_bytes=VMEM_LIMIT),
        name="mixer",
    )(x, ada, g_pre, g_post, g_ffn, w1, b_gate, lb_logits, hg_norm_g, _chunk_tril(t_rows),
      _band_matrices(t_rows), pool_w, pool_scale, w_bh, w_bp, w_out, w_router_t, b_router)


def _route_kernel(eid_ref, cnt_ref, dest_ref, blk_ref, meta_ref, run_ref):
    i = pl.program_id(0)
    cols = eid_ref.shape[1]
    eidx = lax.broadcasted_iota(jnp.int32, (N_EXPERTS, cols), 0)
    hot0 = eidx == eid_ref[0:1, :]
    hot1 = eidx == eid_ref[1:2, :]
    both = jnp.where(hot0 | hot1, 1.0, 0.0)
    tile_cnt = jnp.sum(both, axis=1, keepdims=True)

    @pl.when(i == 0)
    def _():
        total = jnp.sum(cnt_ref[...], axis=0)
        nblk_f = jnp.floor((total + float(MOE_BLOCK - 1)) * (1.0 / MOE_BLOCK))
        er = lax.broadcasted_iota(jnp.int32, (N_EXPERTS, N_EXPERTS), 0)
        ec = lax.broadcasted_iota(jnp.int32, (N_EXPERTS, N_EXPERTS), 1)
        lower = jnp.where(ec < er, 1.0, 0.0).astype(BF16)
        start_blk = jnp.dot(lower, nblk_f.astype(BF16), preferred_element_type=F32)
        run_ref[...] = start_blk[:, 0:1] * float(MOE_BLOCK)
        end_blk = start_blk + nblk_f
        lane = lax.broadcasted_iota(jnp.int32, (N_EXPERTS, blk_ref.shape[1]), 1).astype(F32)
        done = jnp.where(end_blk[:, 0:1] <= lane, 1.0, 0.0)
        blk_ref[...] = jnp.minimum(jnp.sum(done, axis=0, keepdims=True),
                                   float(N_EXPERTS - 1)).astype(jnp.int32)
        meta_ref[...] = jnp.broadcast_to(end_blk[N_EXPERTS - 1:N_EXPERTS, 0:1],
                                         meta_ref.shape).astype(jnp.int32)

    r = lax.broadcasted_iota(jnp.int32, (cols, cols), 0)
    c = lax.broadcasted_iota(jnp.int32, (cols, cols), 1)
    before = jnp.where(r < c, 1.0, 0.0).astype(BF16)
    prefix = jnp.dot(both.astype(BF16), before, preferred_element_type=F32)
    slot = run_ref[...] + prefix
    dest_ref[0:1, :] = jnp.sum(jnp.where(hot0, slot, 0.0), axis=0, keepdims=True).astype(jnp.int32)
    dest_ref[1:2, :] = jnp.sum(jnp.where(hot1, slot, 0.0), axis=0, keepdims=True).astype(jnp.int32)
    run_ref[...] += tile_cnt


def _route(eid, step_counts, n_blocks):
    n_tok = eid.shape[1]
    cols = ROUTE_COLS
    blk_lanes = pl.cdiv(n_blocks, 128) * 128
    return pl.pallas_call(
        _route_kernel,
        grid=(n_tok // cols,),
        in_specs=[pl.BlockSpec((2, cols), lambda i: (0, i)),
                  pl.BlockSpec(step_counts.shape, lambda i: (0, 0, 0))],
        out_specs=[pl.BlockSpec((2, cols), lambda i: (0, i)),
                   pl.BlockSpec((1, blk_lanes), lambda i: (0, 0)),
                   pl.BlockSpec((1, 128), lambda i: (0, 0))],
        out_shape=[jax.ShapeDtypeStruct((2, n_tok), jnp.int32),
                   jax.ShapeDtypeStruct((1, blk_lanes), jnp.int32),
                   jax.ShapeDtypeStruct((1, 128), jnp.int32)],
        scratch_shapes=[pltpu.VMEM((N_EXPERTS, 1), F32)],
        compiler_params=pltpu.CompilerParams(dimension_semantics=("arbitrary",)),
        name="route",
    )(eid, step_counts)


def _row_copy(src_ref, src_row, dst_ref, dst_row, sem):
    return pltpu.make_async_copy(src_ref.at[pl.ds(src_row, 1)], dst_ref.at[pl.ds(dst_row, 1)], sem)


def _drain_rows(rows, wait_pair):
    def drain(g, carry):
        for _ in range(MOVE_UNROLL):
            wait_pair()
        return carry

    lax.fori_loop(0, rows // MOVE_UNROLL, drain, 0)


def _dispatch_kernel(dest_ref, hn2_ref, xs_in_ref, xs_ref, sem):
    del xs_in_ref
    rows = hn2_ref.shape[0]

    def issue(g, carry):
        base = pl.multiple_of(g * MOVE_UNROLL, MOVE_UNROLL)
        group = hn2_ref.at[pl.ds(base, MOVE_UNROLL)]
        for u in range(MOVE_UNROLL):
            _row_copy(group, u, xs_ref, dest_ref[base + u], sem).start(priority=0)
            _row_copy(group, u, xs_ref, dest_ref[rows + base + u], sem).start(priority=1)
        return carry

    lax.fori_loop(0, rows // MOVE_UNROLL, issue, 0)

    def wait_pair():
        _row_copy(hn2_ref, 0, xs_ref, 0, sem).wait()
        _row_copy(hn2_ref, 0, xs_ref, 0, sem).wait()

    _drain_rows(rows, wait_pair)


def _dispatch(dest_tiles, hn2, xs0):
    n_tok, d = hn2.shape
    n_rows_out = xs0.shape[0]
    rows = MOVE_ROWS
    return pl.pallas_call(
        _dispatch_kernel,
        grid=(n_tok // rows,),
        in_specs=[pl.BlockSpec((2 * rows,), lambda i: (i,), memory_space=pltpu.SMEM),
                  pl.BlockSpec((rows, d), lambda i: (i, 0)),
                  pl.BlockSpec(memory_space=pl.ANY)],
        out_specs=pl.BlockSpec(memory_space=pl.ANY),
        out_shape=jax.ShapeDtypeStruct((n_rows_out, d), hn2.dtype),
        scratch_shapes=[pltpu.SemaphoreType.DMA],
        input_output_aliases={2: 0},
        compiler_params=pltpu.CompilerParams(dimension_semantics=("arbitrary",)),
        name="dispatch",
    )(dest_tiles, hn2, xs0)


def _expert_kernel(blk_ref, meta_ref, xs_ref, wg_ref, wu_ref, wd_ref, ys_ref, wg_s, wu_s, wd_s):
    j = pl.program_id(0)
    used = j < meta_ref[0]
    first_of_expert = (j == 0) | (blk_ref[j] != blk_ref[jnp.maximum(j - 1, 0)])

    @pl.when(used & first_of_expert)
    def _():
        wg_s[...] = wg_ref[...].astype(BF16)
        wu_s[...] = wu_ref[...].astype(BF16)
        wd_s[...] = wd_ref[...].astype(BF16)

    @pl.when(used)
    def _():
        half = xs_ref.shape[0] // 2
        halves = [slice(0, half), slice(half, 2 * half)]

        def up_proj(rows):
            xb = _unpack_bf16_pairs(xs_ref[rows, :]).astype(BF16)
            return (jnp.dot(xb, wg_s[...], preferred_element_type=F32),
                    jnp.dot(xb, wu_s[...], preferred_element_type=F32))

        def activate(gp, up):
            return (gp * _sigmoid(gp) * up).astype(BF16)

        def down_proj(rows, act):
            ys_ref[rows, :] = _pack_bf16_pairs(jnp.dot(act, wd_s[...], preferred_element_type=F32))

        gu0 = up_proj(halves[0])
        act0 = activate(*gu0)
        gu1 = up_proj(halves[1])
        down_proj(halves[0], act0)
        act1 = activate(*gu1)
        down_proj(halves[1], act1)

    @pl.when(jnp.logical_not(used))
    def _():
        ys_ref[...] = jnp.zeros_like(ys_ref)


def _experts(block_e, n_used, xs, w_gate, w_up, w_down, n_blocks):
    dp = xs.shape[1]
    ff, d = w_down.shape[1], w_down.shape[2]

    def row_block(j, be, nu):
        return (jnp.minimum(j, nu[0] - 1), 0)

    def expert_block(j, be, nu):
        return (be[jnp.minimum(j, nu[0] - 1)], 0, 0)

    grid_spec = pltpu.PrefetchScalarGridSpec(
        num_scalar_prefetch=2,
        grid=(n_blocks,),
        in_specs=[pl.BlockSpec((MOE_BLOCK, dp), row_block),
                  pl.BlockSpec((None, d, ff), expert_block),
                  pl.BlockSpec((None, d, ff), expert_block),
                  pl.BlockSpec((None, ff, d), expert_block)],
        out_specs=pl.BlockSpec((MOE_BLOCK, dp), lambda j, be, nu: (j, 0)),
        scratch_shapes=[pltpu.VMEM((d, ff), BF16), pltpu.VMEM((d, ff), BF16), pltpu.VMEM((ff, d), BF16)],
    )
    return pl.pallas_call(
        _expert_kernel,
        grid_spec=grid_spec,
        out_shape=jax.ShapeDtypeStruct(xs.shape, xs.dtype),
        compiler_params=pltpu.CompilerParams(dimension_semantics=("arbitrary",),
                                             vmem_limit_bytes=VMEM_LIMIT),
        name="experts",
    )(block_e, n_used, xs, w_gate, w_up, w_down)


def _combine_kernel(dest_ref, ys_ref, rw_ref, h1_ref, ada_ref, g_ref, out_ref, buf0, buf1, sem):
    rows = h1_ref.shape[0]

    def issue(g, carry):
        base = pl.multiple_of(g * MOVE_UNROLL, MOVE_UNROLL)
        group0 = buf0.at[pl.ds(base, MOVE_UNROLL)]
        group1 = buf1.at[pl.ds(base, MOVE_UNROLL)]
        for u in range(MOVE_UNROLL):
            _row_copy(ys_ref, dest_ref[base + u], group0, u, sem).start(priority=0)
            _row_copy(ys_ref, dest_ref[rows + base + u], group1, u, sem).start(priority=1)
        return carry

    lax.fori_loop(0, rows // MOVE_UNROLL, issue, 0)

    def wait_pair():
        _row_copy(ys_ref, 0, buf0, 0, sem).wait()
        _row_copy(ys_ref, 0, buf1, 0, sem).wait()

    _drain_rows(rows, wait_pair)

    rw = rw_ref[...]
    y = rw[:, 0:1] * _unpack_bf16_pairs(buf0[...]) + rw[:, 1:2] * _unpack_bf16_pairs(buf1[...])
    gt2 = ada_ref[5:6, :]
    out_ref[...] = h1_ref[...] + gt2 * _rms_norm(y, g_ref[...])


def _combine(dest_tiles, ys, rw_cols, h1, ada, g_post, seq):
    n_tok, d = h1.shape
    rows = MOVE_ROWS
    per_batch = seq // rows
    return pl.pallas_call(
        _combine_kernel,
        grid=(n_tok // rows,),
        in_specs=[pl.BlockSpec((2 * rows,), lambda i: (i,), memory_space=pltpu.SMEM),
                  pl.BlockSpec(memory_space=pl.ANY),
                  pl.BlockSpec((rows, 2), lambda i: (i, 0)),
                  pl.BlockSpec((rows, d), lambda i: (i, 0)),
                  pl.BlockSpec((None, 6, d), lambda i: (i // per_batch, 0, 0)),
                  pl.BlockSpec((1, d), lambda i: (0, 0))],
        out_specs=pl.BlockSpec((rows, d), lambda i: (i, 0)),
        out_shape=jax.ShapeDtypeStruct((n_tok, d), F32),
        scratch_shapes=[pltpu.VMEM((rows, ys.shape[1]), ys.dtype), pltpu.VMEM((rows, ys.shape[1]), ys.dtype),
                        pltpu.SemaphoreType.DMA],
        compiler_params=pltpu.CompilerParams(dimension_semantics=("arbitrary",),
                                             vmem_limit_bytes=VMEM_LIMIT),
        name="combine",
    )(dest_tiles, ys, rw_cols, h1, ada, g_post)


def kernel(x, c, w_ada, b_ada, g_pre_mix, g_post_mix, w_in, hg_lb_logits, hg_norm_g, pool_w, pool_scale,
           w_branch_hg, w_branch_pool, w_gate, b_gate, w_out, g_pre_ffn, g_post_ffn, w_router_group,
           b_router_group, w_router_expert, b_router_expert, w_exp_gate, w_exp_up, w_exp_down):
    depth = w_in.shape[0]
    bsz, seq, d = x.shape
    n_tok = bsz * seq
    n_blocks = -(-(n_tok * 2) // MOE_BLOCK) + N_EXPERTS
    assert seq % (MIX_ROWS * MIX_SUBS) == 0 and MIX_ROWS % CHUNK == 0 and seq % MOVE_ROWS == 0
    assert n_tok % ROUTE_COLS == 0 and hg_lb_logits.shape[0] == 2 and depth == 1

    h = x
    for l in range(depth):
        ada = _ada(c, w_ada[l], b_ada[l]).reshape(bsz, 6, d)
        w1 = jnp.concatenate([w_in[l], w_gate[l]], axis=1).astype(BF16)
        pad_g = jnp.zeros((d, 8 - MOE_GROUPS), F32)
        pad_e = jnp.zeros((d, ROUTER_COLS - 8 - N_EXPERTS), F32)
        w_router = jnp.concatenate([w_router_group[l], pad_g, w_router_expert[l], pad_e], axis=1)
        w_router_hi = w_router.astype(BF16)
        w_router_lo = (w_router - w_router_hi.astype(F32)).astype(BF16)
        w_router_t = jnp.concatenate([w_router_hi, w_router_lo], axis=1)
        b_router = jnp.concatenate(
            [b_router_group[l], jnp.zeros((8 - MOE_GROUPS,), F32), b_router_expert[l],
             jnp.zeros((ROUTER_COLS - 8 - N_EXPERTS,), F32)])[:, None]
        h1, hn2, eid, rw, step_counts, xs0 = _mixer(
            h, ada, g_pre_mix[l][None], g_post_mix[l][None], g_pre_ffn[l][None], w1, b_gate[l][None],
            hg_lb_logits, hg_norm_g[l][None], pool_w[l].astype(BF16), pool_scale[l][None],
            w_branch_hg[l].astype(BF16), w_branch_pool[l].astype(BF16), w_out[l].astype(BF16),
            w_router_t, b_router, n_blocks * MOE_BLOCK)

        dest, block_e, meta = _route(eid, step_counts, n_blocks)
        n_mt = n_tok // MOVE_ROWS
        dest_tiles = dest.reshape(2, n_mt, MOVE_ROWS).transpose(1, 0, 2).reshape(-1)
        xs = _dispatch(dest_tiles, hn2.reshape(n_tok, d // 2), xs0)
        ys = _experts(block_e[0, :n_blocks], meta[0, :1], xs, w_exp_gate[l], w_exp_up[l], w_exp_down[l],
                      n_blocks)
        h = _combine(dest_tiles, ys, rw.T, h1.reshape(n_tok, d), ada, g_post_ffn[l][None], seq)
        h = h.reshape(bsz, seq, d)
    return h
```

```python
import jax
import jax.numpy as jnp
from jax import lax
from jax.experimental import pallas as pl
from jax.experimental.pallas import tpu as pltpu

F32 = jnp.float32
BF16 = jnp.bfloat16

CHUNK = 64
HG_HEADS = 4
POOL_WINDOWS = (2, 4, 8, 16)
POOL_GROUPS = 4
MOE_GROUPS = 4
MOE_EPG = 8
N_EXPERTS = MOE_GROUPS * MOE_EPG
EPS = 1e-6

MIX_ROWS = 256
MIX_SUBS = 2
POOL_HALO = 128
ROUTE_COLS = 512
MOE_BLOCK = 512
MOVE_ROWS = 1024
MOVE_UNROLL = 8
ROUTER_COLS = 128
LOG_DECAY_SPAN = 80.0
VMEM_LIMIT = 56 * 1024 * 1024

NT_DIMS = (((1,), (1,)), ((), ()))
TN_DIMS = (((0,), (0,)), ((), ()))


def _sigmoid(v):
    return 0.5 * jnp.tanh(0.5 * v) + 0.5


def _rms_norm(v, g):
    return v * lax.rsqrt(jnp.mean(v * v, axis=-1, keepdims=True) + EPS) * g


def _pack_bf16_pairs(v):
    n = v.shape[1] // 2
    lo = lax.bitcast_convert_type(v[:, :n].astype(BF16).astype(F32), jnp.uint32)
    hi = lax.bitcast_convert_type(v[:, n:].astype(BF16).astype(F32), jnp.uint32)
    return hi | (lo >> 16)


def _unpack_bf16_pairs(p):
    lo = lax.bitcast_convert_type(p << 16, F32)
    hi = lax.bitcast_convert_type(p & jnp.uint32(0xFFFF0000), F32)
    return jnp.concatenate([lo, hi], axis=1)


def _ada_kernel(c_ref, w_ref, b_ref, o_ref):
    c = c_ref[...]
    cond = c * _sigmoid(c)
    o_ref[...] = jnp.dot(cond.astype(BF16), w_ref[...].astype(BF16), preferred_element_type=F32) + b_ref[...]


def _ada(c, w_ada, b_ada):
    bsz, d = c.shape
    n_out = w_ada.shape[1]
    return pl.pallas_call(
        _ada_kernel,
        grid=(n_out // d,),
        in_specs=[pl.BlockSpec((bsz, d), lambda i: (0, 0)),
                  pl.BlockSpec((d, d), lambda i: (0, i)),
                  pl.BlockSpec((1, d), lambda i: (0, i))],
        out_specs=pl.BlockSpec((bsz, d), lambda i: (0, i)),
        out_shape=jax.ShapeDtypeStruct((bsz, n_out), F32),
        name="ada",
    )(c, w_ada, b_ada.reshape(1, n_out))


class _Sub:
    pass


def _mix_kernel(x_ref, ada_ref, gpre_ref, gpost_ref, gffn_ref, w1_ref, bgate_ref, lbl_ref, hgn_ref,
                tril_ref, band_ref, poolw_ref, pools_ref, wbh_ref, wbp_ref, wout_ref, wr_ref, br_ref,
                h1_ref, hn2_ref, eid_ref, rw_ref, cnt_ref, blank_ref,
                st_ref, ext_ref, o_scr):
    step_rows, d = x_ref.shape
    t_rows = MIX_ROWS
    hgw = hgn_ref.shape[1]
    hd = hgw // HG_HEADS
    pw = pools_ref.shape[1]
    gd = pw // POOL_GROUPS
    n_chunks = t_rows // CHUNK
    c_u = 4 * hgw
    c_gate = c_u + pw
    j = pl.program_id(1)

    @pl.when(j == 0)
    def _():
        st_ref[...] = jnp.zeros_like(st_ref)
        ext_ref[step_rows:step_rows + POOL_HALO, :] = jnp.zeros((POOL_HALO, pw), BF16)

    ext_ref[0:POOL_HALO, :] = ext_ref[step_rows:step_rows + POOL_HALO, :]

    blank_ref[...] = jnp.zeros(blank_ref.shape, blank_ref.dtype)

    ada = ada_ref[...]
    sh1, sc1, gt1 = ada[0:1], ada[1:2], ada[2:3]
    sh2, sc2 = ada[3:4], ada[4:5]
    lbl = lbl_ref[...]
    lmax = jnp.maximum(lbl[0:1], lbl[1:2])
    e0 = jnp.exp(lbl[0:1] - lmax)
    lb = e0 / (e0 + jnp.exp(lbl[1:2] - lmax))
    row = lax.broadcasted_iota(jnp.int32, (CHUNK, CHUNK), 0)
    col = lax.broadcasted_iota(jnp.int32, (CHUNK, CHUNK), 1)
    causal = row >= col

    def chunk_heads():
        for ci in range(n_chunks):
            for h in range(HG_HEADS):
                yield ci, h, slice(ci * CHUNK, (ci + 1) * CHUNK), slice(h * hd, (h + 1) * hd)

    def m_project(c):
        x = x_ref[c.rows, :]
        hn = _rms_norm(x, gpre_ref[...]) * (1.0 + sc1) + sh1
        c.proj = jnp.dot(hn.astype(BF16), w1_ref[...], preferred_element_type=F32)

    def e_gates(c):
        qr, fr = c.proj[:, 0:hgw], c.proj[:, hgw:2 * hgw]
        vr, gr = c.proj[:, 2 * hgw:3 * hgw], c.proj[:, 3 * hgw:4 * hgw]
        c.q = qr * _sigmoid(qr)
        f = lb + (1.0 - lb) * _sigmoid(fr)
        c.k = 1.0 - f
        lf = jnp.log(f)
        c.lf_hi = lf.astype(BF16)
        c.lf_lo = (lf - c.lf_hi.astype(F32)).astype(BF16)
        c.vb = vr.astype(BF16)
        c.og = hgn_ref[...] * (gr * _sigmoid(gr))
        c.u = c.proj[:, c_u:c_gate]
        ext_ref[c.ext0 + POOL_HALO:c.ext0 + POOL_HALO + t_rows, :] = c.u.astype(BF16)

    def m_cumsum(c):
        tril = tril_ref[...]
        c.b = (jnp.dot(tril, c.lf_hi, preferred_element_type=F32)
               + jnp.dot(tril, c.lf_lo, preferred_element_type=F32))

    def e_decays(c):
        c.qt, c.kt, c.kh, c.dec, c.mid = [], [], [], [], []
        for ci in range(n_chunks):
            rs = slice(ci * CHUNK, (ci + 1) * CHUNK)
            b_c = c.b[rs, :]
            b_mid = b_c[CHUNK // 2 - 1:CHUNK // 2, :]
            b_last = b_c[CHUNK - 1:CHUNK, :]
            rel = jnp.clip(b_c - b_mid, -LOG_DECAY_SPAN, LOG_DECAY_SPAN)
            c.qt.append((c.q[rs, :] * jnp.exp(rel)).astype(BF16))
            c.kt.append((c.k[rs, :] * jnp.exp(-rel)).astype(BF16))
            c.kh.append((c.k[rs, :] * jnp.exp(b_last - b_c)).astype(BF16))
            c.dec.append(jnp.exp(b_last))
            c.mid.append(jnp.exp(b_mid))

    def m_scores(c):
        c.s, c.upd = {}, {}
        for ci, h, rs, cs in chunk_heads():
            c.s[ci, h] = lax.dot_general(c.qt[ci][:, cs], c.kt[ci][:, cs], NT_DIMS, preferred_element_type=F32)
            c.upd[ci, h] = lax.dot_general(c.vb[rs, cs], c.kh[ci][:, cs], TN_DIMS,
                                           preferred_element_type=F32)

    def e_states(c):
        c.sb, c.stb = {}, {}
        for ci, h, rs, cs in chunk_heads():
            c.sb[ci, h] = jnp.where(causal, c.s[ci, h], 0.0).astype(BF16)
        for h in range(HG_HEADS):
            cs = slice(h * hd, (h + 1) * hd)
            st = st_ref[h]
            for ci in range(n_chunks):
                c.stb[ci, h] = (c.mid[ci][:, cs] * st).astype(BF16)
                st = c.dec[ci][:, cs] * st + c.upd[ci, h]
            st_ref[h] = st

    def m_outputs(c):
        c.o = {}
        for ci, h, rs, cs in chunk_heads():
            c.o[ci, h] = (jnp.dot(c.sb[ci, h], c.vb[rs, cs], preferred_element_type=F32)
                          + lax.dot_general(c.qt[ci][:, cs], c.stb[ci, h], NT_DIMS,
                                            preferred_element_type=F32))

    def e_head_norm(c):
        for ci, h, rs, cs in chunk_heads():
            o = c.o[ci, h]
            o = o * lax.rsqrt(jnp.mean(o * o, axis=-1, keepdims=True) + EPS)
            o_scr[c.row0 + ci * CHUNK:c.row0 + (ci + 1) * CHUNK, cs] = (o * c.og[rs, cs]).astype(BF16)

    def m_branch_hg(c):
        c.y_hg = jnp.dot(o_scr[c.rows, :], wbh_ref[...], preferred_element_type=F32)
        ext = ext_ref[c.ext0:c.ext0 + POOL_HALO + t_rows, :]
        c.wsum = [jnp.dot(band_ref[g], ext[:, g * gd:(g + 1) * gd], preferred_element_type=F32)
                  for g in range(POOL_GROUPS)]

    def e_pooled(c):
        pos = j * step_rows + c.row0 + lax.broadcasted_iota(jnp.int32, (t_rows, gd), 0)
        c.pooled = []
        for g in range(POOL_GROUPS):
            cnt = jnp.minimum(pos + 1, POOL_WINDOWS[g]).astype(F32)
            c.pooled.append((c.wsum[g] / cnt - c.u[:, g * gd:(g + 1) * gd]).astype(BF16))

    def m_pool_mix(c):
        c.mixed = [jnp.dot(c.pooled[g], poolw_ref[g], preferred_element_type=F32) for g in range(POOL_GROUPS)]

    def e_gate(c):
        c.mixed = (jnp.concatenate(c.mixed, axis=1) * pools_ref[...]).astype(BF16)
        c.gate = _sigmoid(c.proj[:, c_gate:] + bgate_ref[...])

    def m_branch_pool(c):
        c.y_pool = jnp.dot(c.mixed, wbp_ref[...], preferred_element_type=F32)

    def e_merge(c):
        c.merged = (c.gate[:, 0:d] * c.y_hg + c.gate[:, d:2 * d] * c.y_pool).astype(BF16)

    def m_out(c):
        c.mix = jnp.dot(c.merged, wout_ref[...], preferred_element_type=F32)

    def e_residual(c):
        h1 = x_ref[c.rows, :] + gt1 * _rms_norm(c.mix, gpost_ref[...])
        h1_ref[c.rows, :] = h1
        hn2 = _rms_norm(h1, gffn_ref[...]) * (1.0 + sc2) + sh2
        hn2_ref[c.rows, :] = _pack_bf16_pairs(hn2)
        hn2_hi = hn2.astype(BF16)
        c.hn2_split = jnp.concatenate([hn2_hi, (hn2 - hn2_hi.astype(F32)).astype(BF16)], axis=0)

    def m_router(c):
        c.prod = jnp.dot(c.hn2_split, wr_ref[...], preferred_element_type=F32)

    def e_route(c):
        n_r = wr_ref.shape[1] // 2
        logits = (c.prod[0:t_rows, 0:n_r] + c.prod[t_rows:2 * t_rows, 0:n_r]
                  + c.prod[0:t_rows, n_r:2 * n_r])
        lt = logits.T + br_ref[...]
        lg = lt[0:MOE_GROUPS]
        gmax = jnp.max(lg, axis=0, keepdims=True)
        p_g = 1.0 / jnp.sum(jnp.exp(lg - gmax), axis=0, keepdims=True)
        gi = lax.broadcasted_iota(jnp.int32, lg.shape, 0).astype(F32)
        g_idx = jnp.min(jnp.where(lg == gmax, gi, float(MOE_GROUPS)), axis=0, keepdims=True)
        le = lt[8:8 + MOE_EPG]
        for g in range(1, MOE_GROUPS):
            le = jnp.where(g_idx == float(g), lt[8 + g * MOE_EPG:8 + (g + 1) * MOE_EPG], le)
        ei = lax.broadcasted_iota(jnp.int32, le.shape, 0).astype(F32)
        m1 = jnp.max(le, axis=0, keepdims=True)
        i1 = jnp.min(jnp.where(le == m1, ei, float(MOE_EPG)), axis=0, keepdims=True)
        le2 = jnp.where(ei == i1, -jnp.inf, le)
        m2 = jnp.max(le2, axis=0, keepdims=True)
        i2 = jnp.min(jnp.where(le2 == m2, ei, float(MOE_EPG)), axis=0, keepdims=True)
        r = jnp.exp(m2 - m1)
        w_first = p_g / (1.0 + r)
        e_first, e_second = g_idx * MOE_EPG + i1, g_idx * MOE_EPG + i2
        eid_ref[0:1, c.rows] = e_first.astype(jnp.int32)
        eid_ref[1:2, c.rows] = e_second.astype(jnp.int32)
        ex = lax.broadcasted_iota(jnp.int32, (N_EXPERTS, t_rows), 0).astype(F32)
        hits = jnp.where((ex == e_first) | (ex == e_second), 1.0, 0.0)
        cnt = jnp.broadcast_to(jnp.sum(hits, axis=1, keepdims=True), cnt_ref.shape)
        cnt_ref[...] = cnt if c.row0 == 0 else cnt_ref[...] + cnt
        rw_ref[0:1, c.rows] = w_first
        rw_ref[1:2, c.rows] = w_first * r

    stages = [m_project, e_gates, m_cumsum, e_decays, m_scores, e_states, m_outputs, e_head_norm,
              m_branch_hg, e_pooled, m_pool_mix, e_gate, m_branch_pool, e_merge, m_out, e_residual,
              m_router, e_route]

    subs = []
    for k in range(step_rows // t_rows):
        c = _Sub()
        c.row0 = k * t_rows
        c.rows = slice(c.row0, c.row0 + t_rows)
        c.ext0 = k * t_rows
        subs.append(c)
    for t in range(len(stages) + len(subs) - 1):
        for k, c in enumerate(subs):
            if 0 <= t - k < len(stages):
                stages[t - k](c)


def _band_matrices(t_rows):
    t = jnp.arange(t_rows)[:, None] + POOL_HALO
    jx = jnp.arange(t_rows + POOL_HALO)[None, :]
    return jnp.stack([((jx <= t) & (jx > t - w)) for w in POOL_WINDOWS]).astype(BF16)


def _chunk_tril(t_rows):
    r = jnp.arange(t_rows)[:, None]
    c = jnp.arange(t_rows)[None, :]
    return ((r >= c) & (r // CHUNK == c // CHUNK)).astype(BF16)


def _mixer(x, ada, g_pre, g_post, g_ffn, w1, b_gate, lb_logits, hg_norm_g, pool_w, pool_scale,
           w_bh, w_bp, w_out, w_router_t, b_router, blank_rows):
    bsz, seq, d = x.shape
    t_rows = MIX_ROWS
    step_rows = MIX_ROWS * MIX_SUBS
    n_t = seq // step_rows
    n_tok = bsz * seq
    hgw = hg_norm_g.shape[1]
    pw = pool_scale.shape[1]
    n_r = b_router.shape[0]

    def const(shape):
        return pl.BlockSpec(shape, lambda b, j: (0,) * len(shape), pipeline_mode=pl.Buffered(1))

    in_specs = [
        pl.BlockSpec((None, step_rows, d), lambda b, j: (b, j, 0)),
        pl.BlockSpec((None, 6, d), lambda b, j: (b, 0, 0)),
        const((1, d)), const((1, d)), const((1, d)),
        const(w1.shape), const((1, 2 * d)),
        const(lb_logits.shape), const((1, hgw)),
        const((t_rows, t_rows)), const((POOL_GROUPS, t_rows, t_rows + POOL_HALO)),
        const(pool_w.shape), const((1, pw)),
        const(w_bh.shape), const(w_bp.shape), const(w_out.shape),
        const(w_router_t.shape), const((n_r, 1)),
    ]
    out_specs = [
        pl.BlockSpec((None, step_rows, d), lambda b, j: (b, j, 0)),
        pl.BlockSpec((None, step_rows, d // 2), lambda b, j: (b, j, 0)),
        pl.BlockSpec((2, step_rows), lambda b, j: (0, b * n_t + j)),
        pl.BlockSpec((2, step_rows), lambda b, j: (0, b * n_t + j)),
        pl.BlockSpec((None, N_EXPERTS, 128), lambda b, j: (b * n_t + j, 0, 0)),
        pl.BlockSpec((blank_rows // (bsz * n_t), d // 2), lambda b, j: (b * n_t + j, 0)),
    ]
    out_shape = [
        jax.ShapeDtypeStruct((bsz, seq, d), F32),
        jax.ShapeDtypeStruct((bsz, seq, d // 2), jnp.uint32),
        jax.ShapeDtypeStruct((2, n_tok), jnp.int32),
        jax.ShapeDtypeStruct((2, n_tok), F32),
        jax.ShapeDtypeStruct((bsz * n_t, N_EXPERTS, 128), F32),
        jax.ShapeDtypeStruct((blank_rows, d // 2), jnp.uint32),
    ]
    assert blank_rows % (bsz * n_t * 8) == 0
    scratch = [
        pltpu.VMEM((HG_HEADS, hgw // HG_HEADS, hgw // HG_HEADS), F32),
        pltpu.VMEM((step_rows + POOL_HALO, pw), BF16),
        pltpu.VMEM((step_rows, hgw), BF16),
    ]
    return pl.pallas_call(
        _mix_kernel,
        grid=(bsz, n_t),
        in_specs=in_specs, out_specs=out_specs, out_shape=out_shape, scratch_shapes=scratch,
        compiler_params=pltpu.CompilerParams(dimension_semantics=("arbitrary", "arbitrary"),
                                             vmem_limit_bytes=VMEM_LIMIT),
        name="mixer",
    )(x, ada, g_pre, g_post, g_ffn, w1, b_gate, lb_logits, hg_norm_g, _chunk_tril(t_rows),
      _band_matrices(t_rows), pool_w, pool_scale, w_bh, w_bp, w_out, w_router_t, b_router)


def _route_kernel(eid_ref, cnt_ref, dest_ref, blk_ref, meta_ref, run_ref):
    i = pl.program_id(0)
    cols = eid_ref.shape[1]
    eidx = lax.broadcasted_iota(jnp.int32, (N_EXPERTS, cols), 0)
    hot0 = eidx == eid_ref[0:1, :]
    hot1 = eidx == eid_ref[1:2, :]
    both = jnp.where(hot0 | hot1, 1.0, 0.0)
    tile_cnt = jnp.sum(both, axis=1, keepdims=True)

    @pl.when(i == 0)
    def _():
        total = jnp.sum(cnt_ref[...], axis=0)
        nblk_f = jnp.floor((total + float(MOE_BLOCK - 1)) * (1.0 / MOE_BLOCK))
        er = lax.broadcasted_iota(jnp.int32, (N_EXPERTS, N_EXPERTS), 0)
        ec = lax.broadcasted_iota(jnp.int32, (N_EXPERTS, N_EXPERTS), 1)
        lower = jnp.where(ec < er, 1.0, 0.0).astype(BF16)
        start_blk = jnp.dot(lower, nblk_f.astype(BF16), preferred_element_type=F32)
        run_ref[...] = start_blk[:, 0:1] * float(MOE_BLOCK)
        end_blk = start_blk + nblk_f
        lane = lax.broadcasted_iota(jnp.int32, (N_EXPERTS, blk_ref.shape[1]), 1).astype(F32)
        done = jnp.where(end_blk[:, 0:1] <= lane, 1.0, 0.0)
        blk_ref[...] = jnp.minimum(jnp.sum(done, axis=0, keepdims=True),
                                   float(N_EXPERTS - 1)).astype(jnp.int32)
        meta_ref[...] = jnp.broadcast_to(end_blk[N_EXPERTS - 1:N_EXPERTS, 0:1],
                                         meta_ref.shape).astype(jnp.int32)

    r = lax.broadcasted_iota(jnp.int32, (cols, cols), 0)
    c = lax.broadcasted_iota(jnp.int32, (cols, cols), 1)
    before = jnp.where(r < c, 1.0, 0.0).astype(BF16)
    prefix = jnp.dot(both.astype(BF16), before, preferred_element_type=F32)
    slot = run_ref[...] + prefix
    dest_ref[0:1, :] = jnp.sum(jnp.where(hot0, slot, 0.0), axis=0, keepdims=True).astype(jnp.int32)
    dest_ref[1:2, :] = jnp.sum(jnp.where(hot1, slot, 0.0), axis=0, keepdims=True).astype(jnp.int32)
    run_ref[...] += tile_cnt


def _route(eid, step_counts, n_blocks):
    n_tok = eid.shape[1]
    cols = ROUTE_COLS
    blk_lanes = pl.cdiv(n_blocks, 128) * 128
    return pl.pallas_call(
        _route_kernel,
        grid=(n_tok // cols,),
        in_specs=[pl.BlockSpec((2, cols), lambda i: (0, i)),
                  pl.BlockSpec(step_counts.shape, lambda i: (0, 0, 0))],
        out_specs=[pl.BlockSpec((2, cols), lambda i: (0, i)),
                   pl.BlockSpec((1, blk_lanes), lambda i: (0, 0)),
                   pl.BlockSpec((1, 128), lambda i: (0, 0))],
        out_shape=[jax.ShapeDtypeStruct((2, n_tok), jnp.int32),
                   jax.ShapeDtypeStruct((1, blk_lanes), jnp.int32),
                   jax.ShapeDtypeStruct((1, 128), jnp.int32)],
        scratch_shapes=[pltpu.VMEM((N_EXPERTS, 1), F32)],
        compiler_params=pltpu.CompilerParams(dimension_semantics=("arbitrary",)),
        name="route",
    )(eid, step_counts)


def _row_copy(src_ref, src_row, dst_ref, dst_row, sem):
    return pltpu.make_async_copy(src_ref.at[pl.ds(src_row, 1)], dst_ref.at[pl.ds(dst_row, 1)], sem)


def _drain_rows(rows, wait_pair):
    def drain(g, carry):
        for _ in range(MOVE_UNROLL):
            wait_pair()
        return carry

    lax.fori_loop(0, rows // MOVE_UNROLL, drain, 0)


def _dispatch_kernel(dest_ref, hn2_ref, xs_in_ref, xs_ref, sem):
    del xs_in_ref
    rows = hn2_ref.shape[0]

    def issue(g, carry):
        base = pl.multiple_of(g * MOVE_UNROLL, MOVE_UNROLL)
        group = hn2_ref.at[pl.ds(base, MOVE_UNROLL)]
        for u in range(MOVE_UNROLL):
            _row_copy(group, u, xs_ref, dest_ref[base + u], sem).start(priority=0)
            _row_copy(group, u, xs_ref, dest_ref[rows + base + u], sem).start(priority=1)
        return carry

    lax.fori_loop(0, rows // MOVE_UNROLL, issue, 0)

    def wait_pair():
        _row_copy(hn2_ref, 0, xs_ref, 0, sem).wait()
        _row_copy(hn2_ref, 0, xs_ref, 0, sem).wait()

    _drain_rows(rows, wait_pair)


def _dispatch(dest_tiles, hn2, xs0):
    n_tok, d = hn2.shape
    n_rows_out = xs0.shape[0]
    rows = MOVE_ROWS
    return pl.pallas_call(
        _dispatch_kernel,
        grid=(n_tok // rows,),
        in_specs=[pl.BlockSpec((2 * rows,), lambda i: (i,), memory_space=pltpu.SMEM),
                  pl.BlockSpec((rows, d), lambda i: (i, 0)),
                  pl.BlockSpec(memory_space=pl.ANY)],
        out_specs=pl.BlockSpec(memory_space=pl.ANY),
        out_shape=jax.ShapeDtypeStruct((n_rows_out, d), hn2.dtype),
        scratch_shapes=[pltpu.SemaphoreType.DMA],
        input_output_aliases={2: 0},
        compiler_params=pltpu.CompilerParams(dimension_semantics=("arbitrary",)),
        name="dispatch",
    )(dest_tiles, hn2, xs0)


def _expert_kernel(blk_ref, meta_ref, xs_ref, wg_ref, wu_ref, wd_ref, ys_ref, wg_s, wu_s, wd_s):
    j = pl.program_id(0)
    used = j < meta_ref[0]
    first_of_expert = (j == 0) | (blk_ref[j] != blk_ref[jnp.maximum(j - 1, 0)])

    @pl.when(used & first_of_expert)
    def _():
        wg_s[...] = wg_ref[...].astype(BF16)
        wu_s[...] = wu_ref[...].astype(BF16)
        wd_s[...] = wd_ref[...].astype(BF16)

    @pl.when(used)
    def _():
        half = xs_ref.shape[0] // 2
        halves = [slice(0, half), slice(half, 2 * half)]

        def up_proj(rows):
            xb = _unpack_bf16_pairs(xs_ref[rows, :]).astype(BF16)
            return (jnp.dot(xb, wg_s[...], preferred_element_type=F32),
                    jnp.dot(xb, wu_s[...], preferred_element_type=F32))

        def activate(gp, up):
            return (gp * _sigmoid(gp) * up).astype(BF16)

        def down_proj(rows, act):
            ys_ref[rows, :] = _pack_bf16_pairs(jnp.dot(act, wd_s[...], preferred_element_type=F32))

        gu0 = up_proj(halves[0])
        act0 = activate(*gu0)
        gu1 = up_proj(halves[1])
        down_proj(halves[0], act0)
        act1 = activate(*gu1)
        down_proj(halves[1], act1)

    @pl.when(jnp.logical_not(used))
    def _():
        ys_ref[...] = jnp.zeros_like(ys_ref)


def _experts(block_e, n_used, xs, w_gate, w_up, w_down, n_blocks):
    dp = xs.shape[1]
    ff, d = w_down.shape[1], w_down.shape[2]

    def row_block(j, be, nu):
        return (jnp.minimum(j, nu[0] - 1), 0)

    def expert_block(j, be, nu):
        return (be[jnp.minimum(j, nu[0] - 1)], 0, 0)

    grid_spec = pltpu.PrefetchScalarGridSpec(
        num_scalar_prefetch=2,
        grid=(n_blocks,),
        in_specs=[pl.BlockSpec((MOE_BLOCK, dp), row_block),
                  pl.BlockSpec((None, d, ff), expert_block),
                  pl.BlockSpec((None, d, ff), expert_block),
                  pl.BlockSpec((None, ff, d), expert_block)],
        out_specs=pl.BlockSpec((MOE_BLOCK, dp), lambda j, be, nu: (j, 0)),
        scratch_shapes=[pltpu.VMEM((d, ff), BF16), pltpu.VMEM((d, ff), BF16), pltpu.VMEM((ff, d), BF16)],
    )
    return pl.pallas_call(
        _expert_kernel,
        grid_spec=grid_spec,
        out_shape=jax.ShapeDtypeStruct(xs.shape, xs.dtype),
        compiler_params=pltpu.CompilerParams(dimension_semantics=("arbitrary",),
                                             vmem_limit_bytes=VMEM_LIMIT),
        name="experts",
    )(block_e, n_used, xs, w_gate, w_up, w_down)


def _combine_kernel(dest_ref, ys_ref, rw_ref, h1_ref, ada_ref, g_ref, out_ref, buf0, buf1, sem):
    rows = h1_ref.shape[0]

    def issue(g, carry):
        base = pl.multiple_of(g * MOVE_UNROLL, MOVE_UNROLL)
        group0 = buf0.at[pl.ds(base, MOVE_UNROLL)]
        group1 = buf1.at[pl.ds(base, MOVE_UNROLL)]
        for u in range(MOVE_UNROLL):
            _row_copy(ys_ref, dest_ref[base + u], group0, u, sem).start(priority=0)
            _row_copy(ys_ref, dest_ref[rows + base + u], group1, u, sem).start(priority=1)
        return carry

    lax.fori_loop(0, rows // MOVE_UNROLL, issue, 0)

    def wait_pair():
        _row_copy(ys_ref, 0, buf0, 0, sem).wait()
        _row_copy(ys_ref, 0, buf1, 0, sem).wait()

    _drain_rows(rows, wait_pair)

    rw = rw_ref[...]
    y = rw[:, 0:1] * _unpack_bf16_pairs(buf0[...]) + rw[:, 1:2] * _unpack_bf16_pairs(buf1[...])
    gt2 = ada_ref[5:6, :]
    out_ref[...] = h1_ref[...] + gt2 * _rms_norm(y, g_ref[...])


def _combine(dest_tiles, ys, rw_cols, h1, ada, g_post, seq):
    n_tok, d = h1.shape
    rows = MOVE_ROWS
    per_batch = seq // rows
    return pl.pallas_call(
        _combine_kernel,
        grid=(n_tok // rows,),
        in_specs=[pl.BlockSpec((2 * rows,), lambda i: (i,), memory_space=pltpu.SMEM),
                  pl.BlockSpec(memory_space=pl.ANY),
                  pl.BlockSpec((rows, 2), lambda i: (i, 0)),
                  pl.BlockSpec((rows, d), lambda i: (i, 0)),
                  pl.BlockSpec((None, 6, d), lambda i: (i // per_batch, 0, 0)),
                  pl.BlockSpec((1, d), lambda i: (0, 0))],
        out_specs=pl.BlockSpec((rows, d), lambda i: (i, 0)),
        out_shape=jax.ShapeDtypeStruct((n_tok, d), F32),
        scratch_shapes=[pltpu.VMEM((rows, ys.shape[1]), ys.dtype), pltpu.VMEM((rows, ys.shape[1]), ys.dtype),
                        pltpu.SemaphoreType.DMA],
        compiler_params=pltpu.CompilerParams(dimension_semantics=("arbitrary",),
                                             vmem_limit_bytes=VMEM_LIMIT),
        name="combine",
    )(dest_tiles, ys, rw_cols, h1, ada, g_post)


def kernel(x, c, w_ada, b_ada, g_pre_mix, g_post_mix, w_in, hg_lb_logits, hg_norm_g, pool_w, pool_scale,
           w_branch_hg, w_branch_pool, w_gate, b_gate, w_out, g_pre_ffn, g_post_ffn, w_router_group,
           b_router_group, w_router_expert, b_router_expert, w_exp_gate, w_exp_up, w_exp_down):
    depth = w_in.shape[0]
    bsz, seq, d = x.shape
    n_tok = bsz * seq
    n_blocks = -(-(n_tok * 2) // MOE_BLOCK) + N_EXPERTS
    assert seq % (MIX_ROWS * MIX_SUBS) == 0 and MIX_ROWS % CHUNK == 0 and seq % MOVE_ROWS == 0
    assert n_tok % ROUTE_COLS == 0 and hg_lb_logits.shape[0] == 2 and depth == 1

    h = x
    for l in range(depth):
        ada = _ada(c, w_ada[l], b_ada[l]).reshape(bsz, 6, d)
        w1 = jnp.concatenate([w_in[l], w_gate[l]], axis=1).astype(BF16)
        pad_g = jnp.zeros((d, 8 - MOE_GROUPS), F32)
        pad_e = jnp.zeros((d, ROUTER_COLS - 8 - N_EXPERTS), F32)
        w_router = jnp.concatenate([w_router_group[l], pad_g, w_router_expert[l], pad_e], axis=1)
        w_router_hi = w_router.astype(BF16)
        w_router_lo = (w_router - w_router_hi.astype(F32)).astype(BF16)
        w_router_t = jnp.concatenate([w_router_hi, w_router_lo], axis=1)
        b_router = jnp.concatenate(
            [b_router_group[l], jnp.zeros((8 - MOE_GROUPS,), F32), b_router_expert[l],
             jnp.zeros((ROUTER_COLS - 8 - N_EXPERTS,), F32)])[:, None]
        h1, hn2, eid, rw, step_counts, xs0 = _mixer(
            h, ada, g_pre_mix[l][None], g_post_mix[l][None], g_pre_ffn[l][None], w1, b_gate[l][None],
            hg_lb_logits, hg_norm_g[l][None], pool_w[l].astype(BF16), pool_scale[l][None],
            w_branch_hg[l].astype(BF16), w_branch_pool[l].astype(BF16), w_out[l].astype(BF16),
            w_router_t, b_router, n_blocks * MOE_BLOCK)

        dest, block_e, meta = _route(eid, step_counts, n_blocks)
        n_mt = n_tok // MOVE_ROWS
        dest_tiles = dest.reshape(2, n_mt, MOVE_ROWS).transpose(1, 0, 2).reshape(-1)
        xs = _dispatch(dest_tiles, hn2.reshape(n_tok, d // 2), xs0)
        ys = _experts(block_e[0, :n_blocks], meta[0, :1], xs, w_exp_gate[l], w_exp_up[l], w_exp_down[l],
                      n_blocks)
        h = _combine(dest_tiles, ys, rw.T, h1.reshape(n_tok, d), ada, g_post_ffn[l][None], seq)
        h = h.reshape(bsz, seq, d)
    return h
```

```python
import jax
import jax.numpy as jnp
from jax import lax
from jax.experimental import pallas as pl
from jax.experimental.pallas import tpu as pltpu
from jax.experimental.pallas import tpu_sc as plsc

F32 = jnp.float32
BF16 = jnp.bfloat16

CHUNK = 64
HG_HEADS = 4
POOL_WINDOWS = (2, 4, 8, 16)
POOL_GROUPS = 4
MOE_GROUPS = 4
MOE_EPG = 8
N_EXPERTS = MOE_GROUPS * MOE_EPG
EPS = 1e-6

MIX_ROWS = 256
MIX_SUBS = 2
POOL_HALO = 128
ROUTE_COLS = 512
MOE_BLOCK = 512
MOVE_ROWS = 1024
MOVE_UNROLL = 8
SC_WINDOW = 128
ROUTER_COLS = 128
LOG_DECAY_SPAN = 80.0
VMEM_LIMIT = 56 * 1024 * 1024

NT_DIMS = (((1,), (1,)), ((), ()))
TN_DIMS = (((0,), (0,)), ((), ()))


def _sigmoid(v):
    return 0.5 * jnp.tanh(0.5 * v) + 0.5


def _rms_norm(v, g):
    return v * lax.rsqrt(jnp.mean(v * v, axis=-1, keepdims=True) + EPS) * g


def _pack_bf16_pairs(v):
    n = v.shape[1] // 2
    lo = lax.bitcast_convert_type(v[:, :n].astype(BF16).astype(F32), jnp.uint32)
    hi = lax.bitcast_convert_type(v[:, n:].astype(BF16).astype(F32), jnp.uint32)
    return hi | (lo >> 16)


def _unpack_bf16_pairs(p):
    lo = lax.bitcast_convert_type(p << 16, F32)
    hi = lax.bitcast_convert_type(p & jnp.uint32(0xFFFF0000), F32)
    return jnp.concatenate([lo, hi], axis=1)


def _ada_kernel(c_ref, w_ref, b_ref, o_ref):
    c = c_ref[...]
    cond = c * _sigmoid(c)
    o_ref[...] = jnp.dot(cond.astype(BF16), w_ref[...].astype(BF16), preferred_element_type=F32) + b_ref[...]


def _ada(c, w_ada, b_ada):
    bsz, d = c.shape
    n_out = w_ada.shape[1]
    return pl.pallas_call(
        _ada_kernel,
        grid=(n_out // d,),
        in_specs=[pl.BlockSpec((bsz, d), lambda i: (0, 0)),
                  pl.BlockSpec((d, d), lambda i: (0, i)),
                  pl.BlockSpec((1, d), lambda i: (0, i))],
        out_specs=pl.BlockSpec((bsz, d), lambda i: (0, i)),
        out_shape=jax.ShapeDtypeStruct((bsz, n_out), F32),
        name="ada",
    )(c, w_ada, b_ada.reshape(1, n_out))


class _Sub:
    pass


def _mix_kernel(x_ref, ada_ref, gpre_ref, gpost_ref, gffn_ref, w1_ref, bgate_ref, lbl_ref, hgn_ref,
                tril_ref, band_ref, poolw_ref, pools_ref, wbh_ref, wbp_ref, wout_ref, wr_ref, br_ref,
                h1_ref, hn2_ref, eid_ref, rw_ref, cnt_ref, blank_ref,
                st_ref, ext_ref, o_scr):
    step_rows, d = x_ref.shape
    t_rows = MIX_ROWS
    hgw = hgn_ref.shape[1]
    hd = hgw // HG_HEADS
    pw = pools_ref.shape[1]
    gd = pw // POOL_GROUPS
    n_chunks = t_rows // CHUNK
    c_u = 4 * hgw
    c_gate = c_u + pw
    j = pl.program_id(1)

    @pl.when(j == 0)
    def _():
        st_ref[...] = jnp.zeros_like(st_ref)
        ext_ref[step_rows:step_rows + POOL_HALO, :] = jnp.zeros((POOL_HALO, pw), BF16)

    ext_ref[0:POOL_HALO, :] = ext_ref[step_rows:step_rows + POOL_HALO, :]

    blank_ref[...] = jnp.zeros(blank_ref.shape, blank_ref.dtype)

    ada = ada_ref[...]
    sh1, sc1, gt1 = ada[0:1], ada[1:2], ada[2:3]
    sh2, sc2 = ada[3:4], ada[4:5]
    lbl = lbl_ref[...]
    lmax = jnp.maximum(lbl[0:1], lbl[1:2])
    e0 = jnp.exp(lbl[0:1] - lmax)
    lb = e0 / (e0 + jnp.exp(lbl[1:2] - lmax))
    row = lax.broadcasted_iota(jnp.int32, (CHUNK, CHUNK), 0)
    col = lax.broadcasted_iota(jnp.int32, (CHUNK, CHUNK), 1)
    causal = row >= col

    def chunk_heads():
        for ci in range(n_chunks):
            for h in range(HG_HEADS):
                yield ci, h, slice(ci * CHUNK, (ci + 1) * CHUNK), slice(h * hd, (h + 1) * hd)

    def m_project(c):
        x = x_ref[c.rows, :]
        hn = _rms_norm(x, gpre_ref[...]) * (1.0 + sc1) + sh1
        c.proj = jnp.dot(hn.astype(BF16), w1_ref[...], preferred_element_type=F32)

    def e_gates(c):
        qr, fr = c.proj[:, 0:hgw], c.proj[:, hgw:2 * hgw]
        vr, gr = c.proj[:, 2 * hgw:3 * hgw], c.proj[:, 3 * hgw:4 * hgw]
        c.q = qr * _sigmoid(qr)
        f = lb + (1.0 - lb) * _sigmoid(fr)
        c.k = 1.0 - f
        lf = jnp.log(f)
        c.lf_hi = lf.astype(BF16)
        c.lf_lo = (lf - c.lf_hi.astype(F32)).astype(BF16)
        c.vb = vr.astype(BF16)
        c.og = hgn_ref[...] * (gr * _sigmoid(gr))
        c.u = c.proj[:, c_u:c_gate]
        ext_ref[c.ext0 + POOL_HALO:c.ext0 + POOL_HALO + t_rows, :] = c.u.astype(BF16)

    def m_cumsum(c):
        tril = tril_ref[...]
        c.b = (jnp.dot(tril, c.lf_hi, preferred_element_type=F32)
               + jnp.dot(tril, c.lf_lo, preferred_element_type=F32))

    def e_decays(c):
        c.qt, c.kt, c.kh, c.dec, c.mid = [], [], [], [], []
        for ci in range(n_chunks):
            rs = slice(ci * CHUNK, (ci + 1) * CHUNK)
            b_c = c.b[rs, :]
            b_mid = b_c[CHUNK // 2 - 1:CHUNK // 2, :]
            b_last = b_c[CHUNK - 1:CHUNK, :]
            rel = jnp.clip(b_c - b_mid, -LOG_DECAY_SPAN, LOG_DECAY_SPAN)
            c.qt.append((c.q[rs, :] * jnp.exp(rel)).astype(BF16))
            c.kt.append((c.k[rs, :] * jnp.exp(-rel)).astype(BF16))
            c.kh.append((c.k[rs, :] * jnp.exp(b_last - b_c)).astype(BF16))
            c.dec.append(jnp.exp(b_last))
            c.mid.append(jnp.exp(b_mid))

    def m_scores(c):
        c.s, c.upd = {}, {}
        for ci, h, rs, cs in chunk_heads():
            c.s[ci, h] = lax.dot_general(c.qt[ci][:, cs], c.kt[ci][:, cs], NT_DIMS, preferred_element_type=F32)
            c.upd[ci, h] = lax.dot_general(c.vb[rs, cs], c.kh[ci][:, cs], TN_DIMS,
                                           preferred_element_type=F32)

    def e_states(c):
        c.sb, c.stb = {}, {}
        for ci, h, rs, cs in chunk_heads():
            c.sb[ci, h] = jnp.where(causal, c.s[ci, h], 0.0).astype(BF16)
        for h in range(HG_HEADS):
            cs = slice(h * hd, (h + 1) * hd)
            st = st_ref[h]
            for ci in range(n_chunks):
                c.stb[ci, h] = (c.mid[ci][:, cs] * st).astype(BF16)
                st = c.dec[ci][:, cs] * st + c.upd[ci, h]
            st_ref[h] = st

    def m_outputs(c):
        c.o = {}
        for ci, h, rs, cs in chunk_heads():
            c.o[ci, h] = (jnp.dot(c.sb[ci, h], c.vb[rs, cs], preferred_element_type=F32)
                          + lax.dot_general(c.qt[ci][:, cs], c.stb[ci, h], NT_DIMS,
                                            preferred_element_type=F32))

    def e_head_norm(c):
        for ci, h, rs, cs in chunk_heads():
            o = c.o[ci, h]
            o = o * lax.rsqrt(jnp.mean(o * o, axis=-1, keepdims=True) + EPS)
            o_scr[c.row0 + ci * CHUNK:c.row0 + (ci + 1) * CHUNK, cs] = (o * c.og[rs, cs]).astype(BF16)

    def m_branch_hg(c):
        c.y_hg = jnp.dot(o_scr[c.rows, :], wbh_ref[...], preferred_element_type=F32)
        ext = ext_ref[c.ext0:c.ext0 + POOL_HALO + t_rows, :]
        c.wsum = [jnp.dot(band_ref[g], ext[:, g * gd:(g + 1) * gd], preferred_element_type=F32)
                  for g in range(POOL_GROUPS)]

    def e_pooled(c):
        pos = j * step_rows + c.row0 + lax.broadcasted_iota(jnp.int32, (t_rows, gd), 0)
        c.pooled = []
        for g in range(POOL_GROUPS):
            cnt = jnp.minimum(pos + 1, POOL_WINDOWS[g]).astype(F32)
            c.pooled.append((c.wsum[g] / cnt - c.u[:, g * gd:(g + 1) * gd]).astype(BF16))

    def m_pool_mix(c):
        c.mixed = [jnp.dot(c.pooled[g], poolw_ref[g], preferred_element_type=F32) for g in range(POOL_GROUPS)]

    def e_gate(c):
        c.mixed = (jnp.concatenate(c.mixed, axis=1) * pools_ref[...]).astype(BF16)
        c.gate = _sigmoid(c.proj[:, c_gate:] + bgate_ref[...])

    def m_branch_pool(c):
        c.y_pool = jnp.dot(c.mixed, wbp_ref[...], preferred_element_type=F32)

    def e_merge(c):
        c.merged = (c.gate[:, 0:d] * c.y_hg + c.gate[:, d:2 * d] * c.y_pool).astype(BF16)

    def m_out(c):
        c.mix = jnp.dot(c.merged, wout_ref[...], preferred_element_type=F32)

    def e_residual(c):
        h1 = x_ref[c.rows, :] + gt1 * _rms_norm(c.mix, gpost_ref[...])
        h1_ref[c.rows, :] = h1
        hn2 = _rms_norm(h1, gffn_ref[...]) * (1.0 + sc2) + sh2
        hn2_ref[c.rows, :] = _pack_bf16_pairs(hn2)
        hn2_hi = hn2.astype(BF16)
        c.hn2_split = jnp.concatenate([hn2_hi, (hn2 - hn2_hi.astype(F32)).astype(BF16)], axis=0)

    def m_router(c):
        c.prod = jnp.dot(c.hn2_split, wr_ref[...], preferred_element_type=F32)

    def e_route(c):
        n_r = wr_ref.shape[1] // 2
        logits = (c.prod[0:t_rows, 0:n_r] + c.prod[t_rows:2 * t_rows, 0:n_r]
                  + c.prod[0:t_rows, n_r:2 * n_r])
        lt = logits.T + br_ref[...]
        lg = lt[0:MOE_GROUPS]
        gmax = jnp.max(lg, axis=0, keepdims=True)
        p_g = 1.0 / jnp.sum(jnp.exp(lg - gmax), axis=0, keepdims=True)
        gi = lax.broadcasted_iota(jnp.int32, lg.shape, 0).astype(F32)
        g_idx = jnp.min(jnp.where(lg == gmax, gi, float(MOE_GROUPS)), axis=0, keepdims=True)
        le = lt[8:8 + MOE_EPG]
        for g in range(1, MOE_GROUPS):
            le = jnp.where(g_idx == float(g), lt[8 + g * MOE_EPG:8 + (g + 1) * MOE_EPG], le)
        ei = lax.broadcasted_iota(jnp.int32, le.shape, 0).astype(F32)
        m1 = jnp.max(le, axis=0, keepdims=True)
        i1 = jnp.min(jnp.where(le == m1, ei, float(MOE_EPG)), axis=0, keepdims=True)
        le2 = jnp.where(ei == i1, -jnp.inf, le)
        m2 = jnp.max(le2, axis=0, keepdims=True)
        i2 = jnp.min(jnp.where(le2 == m2, ei, float(MOE_EPG)), axis=0, keepdims=True)
        r = jnp.exp(m2 - m1)
        w_first = p_g / (1.0 + r)
        e_first, e_second = g_idx * MOE_EPG + i1, g_idx * MOE_EPG + i2
        eid_ref[0:1, c.rows] = e_first.astype(jnp.int32)
        eid_ref[1:2, c.rows] = e_second.astype(jnp.int32)
        ex = lax.broadcasted_iota(jnp.int32, (N_EXPERTS, t_rows), 0).astype(F32)
        hits = jnp.where((ex == e_first) | (ex == e_second), 1.0, 0.0)
        cnt = jnp.broadcast_to(jnp.sum(hits, axis=1, keepdims=True), cnt_ref.shape)
        cnt_ref[...] = cnt if c.row0 == 0 else cnt_ref[...] + cnt
        rw_ref[0:1, c.rows] = w_first
        rw_ref[1:2, c.rows] = w_first * r

    stages = [m_project, e_gates, m_cumsum, e_decays, m_scores, e_states, m_outputs, e_head_norm,
              m_branch_hg, e_pooled, m_pool_mix, e_gate, m_branch_pool, e_merge, m_out, e_residual,
              m_router, e_route]

    subs = []
    for k in range(step_rows // t_rows):
        c = _Sub()
        c.row0 = k * t_rows
        c.rows = slice(c.row0, c.row0 + t_rows)
        c.ext0 = k * t_rows
        subs.append(c)
    for t in range(len(stages) + len(subs) - 1):
        for k, c in enumerate(subs):
            if 0 <= t - k < len(stages):
                stages[t - k](c)


def _band_matrices(t_rows):
    t = jnp.arange(t_rows)[:, None] + POOL_HALO
    jx = jnp.arange(t_rows + POOL_HALO)[None, :]
    return jnp.stack([((jx <= t) & (jx > t - w)) for w in POOL_WINDOWS]).astype(BF16)


def _chunk_tril(t_rows):
    r = jnp.arange(t_rows)[:, None]
    c = jnp.arange(t_rows)[None, :]
    return ((r >= c) & (r // CHUNK == c // CHUNK)).astype(BF16)


def _mixer(x, ada, g_pre, g_post, g_ffn, w1, b_gate, lb_logits, hg_norm_g, pool_w, pool_scale,
           w_bh, w_bp, w_out, w_router_t, b_router, blank_rows):
    bsz, seq, d = x.shape
    t_rows = MIX_ROWS
    step_rows = MIX_ROWS * MIX_SUBS
    n_t = seq // step_rows
    n_tok = bsz * seq
    hgw = hg_norm_g.shape[1]
    pw = pool_scale.shape[1]
    n_r = b_router.shape[0]

    def const(shape):
        return pl.BlockSpec(shape, lambda b, j: (0,) * len(shape), pipeline_mode=pl.Buffered(1))

    in_specs = [
        pl.BlockSpec((None, step_rows, d), lambda b, j: (b, j, 0)),
        pl.BlockSpec((None, 6, d), lambda b, j: (b, 0, 0)),
        const((1, d)), const((1, d)), const((1, d)),
        const(w1.shape), const((1, 2 * d)),
        const(lb_logits.shape), const((1, hgw)),
        const((t_rows, t_rows)), const((POOL_GROUPS, t_rows, t_rows + POOL_HALO)),
        const(pool_w.shape), const((1, pw)),
        const(w_bh.shape), const(w_bp.shape), const(w_out.shape),
        const(w_router_t.shape), const((n_r, 1)),
    ]
    out_specs = [
        pl.BlockSpec((None, step_rows, d), lambda b, j: (b, j, 0)),
        pl.BlockSpec((None, step_rows, d // 2), lambda b, j: (b, j, 0)),
        pl.BlockSpec((2, step_rows), lambda b, j: (0, b * n_t + j)),
        pl.BlockSpec((2, step_rows), lambda b, j: (0, b * n_t + j)),
        pl.BlockSpec((None, N_EXPERTS, 128), lambda b, j: (b * n_t + j, 0, 0)),
        pl.BlockSpec((blank_rows // (bsz * n_t), d // 2), lambda b, j: (b * n_t + j, 0)),
    ]
    out_shape = [
        jax.ShapeDtypeStruct((bsz, seq, d), F32),
        jax.ShapeDtypeStruct((bsz, seq, d // 2), jnp.uint32),
        jax.ShapeDtypeStruct((2, n_tok), jnp.int32),
        jax.ShapeDtypeStruct((2, n_tok), F32),
        jax.ShapeDtypeStruct((bsz * n_t, N_EXPERTS, 128), F32),
        jax.ShapeDtypeStruct((blank_rows, d // 2), jnp.uint32),
    ]
    assert blank_rows % (bsz * n_t * 8) == 0
    scratch = [
        pltpu.VMEM((HG_HEADS, hgw // HG_HEADS, hgw // HG_HEADS), F32),
        pltpu.VMEM((step_rows + POOL_HALO, pw), BF16),
        pltpu.VMEM((step_rows, hgw), BF16),
    ]
    return pl.pallas_call(
        _mix_kernel,
        grid=(bsz, n_t),
        in_specs=in_specs, out_specs=out_specs, out_shape=out_shape, scratch_shapes=scratch,
        compiler_params=pltpu.CompilerParams(dimension_semantics=("arbitrary", "arbitrary"),
                                             vmem_limit_bytes=VMEM_LIMIT),
        name="mixer",
    )(x, ada, g_pre, g_post, g_ffn, w1, b_gate, lb_logits, hg_norm_g, _chunk_tril(t_rows),
      _band_matrices(t_rows), pool_w, pool_scale, w_bh, w_bp, w_out, w_router_t, b_router)


def _route_kernel(eid_ref, cnt_ref, dest_ref, blk_ref, meta_ref, run_ref):
    i = pl.program_id(0)
    cols = eid_ref.shape[1]
    eidx = lax.broadcasted_iota(jnp.int32, (N_EXPERTS, cols), 0)
    hot0 = eidx == eid_ref[0:1, :]
    hot1 = eidx == eid_ref[1:2, :]
    both = jnp.where(hot0 | hot1, 1.0, 0.0)
    tile_cnt = jnp.sum(both, axis=1, keepdims=True)

    @pl.when(i == 0)
    def _():
        total = jnp.sum(cnt_ref[...], axis=0)
        nblk_f = jnp.floor((total + float(MOE_BLOCK - 1)) * (1.0 / MOE_BLOCK))
        er = lax.broadcasted_iota(jnp.int32, (N_EXPERTS, N_EXPERTS), 0)
        ec = lax.broadcasted_iota(jnp.int32, (N_EXPERTS, N_EXPERTS), 1)
        lower = jnp.where(ec < er, 1.0, 0.0).astype(BF16)
        start_blk = jnp.dot(lower, nblk_f.astype(BF16), preferred_element_type=F32)
        run_ref[...] = start_blk[:, 0:1] * float(MOE_BLOCK)
        end_blk = start_blk + nblk_f
        lane = lax.broadcasted_iota(jnp.int32, (N_EXPERTS, blk_ref.shape[1]), 1).astype(F32)
        done = jnp.where(end_blk[:, 0:1] <= lane, 1.0, 0.0)
        blk_ref[...] = jnp.minimum(jnp.sum(done, axis=0, keepdims=True),
                                   float(N_EXPERTS - 1)).astype(jnp.int32)
        meta_ref[...] = jnp.broadcast_to(end_blk[N_EXPERTS - 1:N_EXPERTS, 0:1],
                                         meta_ref.shape).astype(jnp.int32)

    r = lax.broadcasted_iota(jnp.int32, (cols, cols), 0)
    c = lax.broadcasted_iota(jnp.int32, (cols, cols), 1)
    before = jnp.where(r < c, 1.0, 0.0).astype(BF16)
    prefix = jnp.dot(both.astype(BF16), before, preferred_element_type=F32)
    slot = run_ref[...] + prefix
    dest_ref[0:1, :] = jnp.sum(jnp.where(hot0, slot, 0.0), axis=0, keepdims=True).astype(jnp.int32)
    dest_ref[1:2, :] = jnp.sum(jnp.where(hot1, slot, 0.0), axis=0, keepdims=True).astype(jnp.int32)
    run_ref[...] += tile_cnt


def _route(eid, step_counts, n_blocks):
    n_tok = eid.shape[1]
    cols = ROUTE_COLS
    blk_lanes = pl.cdiv(n_blocks, 128) * 128
    return pl.pallas_call(
        _route_kernel,
        grid=(n_tok // cols,),
        in_specs=[pl.BlockSpec((2, cols), lambda i: (0, i)),
                  pl.BlockSpec(step_counts.shape, lambda i: (0, 0, 0))],
        out_specs=[pl.BlockSpec((2, cols), lambda i: (0, i)),
                   pl.BlockSpec((1, blk_lanes), lambda i: (0, 0)),
                   pl.BlockSpec((1, 128), lambda i: (0, 0))],
        out_shape=[jax.ShapeDtypeStruct((2, n_tok), jnp.int32),
                   jax.ShapeDtypeStruct((1, blk_lanes), jnp.int32),
                   jax.ShapeDtypeStruct((1, 128), jnp.int32)],
        scratch_shapes=[pltpu.VMEM((N_EXPERTS, 1), F32)],
        compiler_params=pltpu.CompilerParams(dimension_semantics=("arbitrary",)),
        name="route",
    )(eid, step_counts)


def _row_copy(src_ref, src_row, dst_ref, dst_row, sem):
    return pltpu.make_async_copy(src_ref.at[pl.ds(src_row, 1)], dst_ref.at[pl.ds(dst_row, 1)], sem)


def _drain_rows(rows, wait_pair):
    def drain(g, carry):
        for _ in range(MOVE_UNROLL):
            wait_pair()
        return carry

    lax.fori_loop(0, rows // MOVE_UNROLL, drain, 0)


def _dispatch_kernel(dest_ref, hn2_ref, xs_in_ref, xs_ref, sem):
    del xs_in_ref
    rows = hn2_ref.shape[0]

    def issue(g, carry):
        base = pl.multiple_of(g * MOVE_UNROLL, MOVE_UNROLL)
        group = hn2_ref.at[pl.ds(base, MOVE_UNROLL)]
        for u in range(MOVE_UNROLL):
            _row_copy(group, u, xs_ref, dest_ref[base + u], sem).start(priority=0)
            _row_copy(group, u, xs_ref, dest_ref[rows + base + u], sem).start(priority=1)
        return carry

    lax.fori_loop(0, rows // MOVE_UNROLL, issue, 0)

    def wait_pair():
        _row_copy(hn2_ref, 0, xs_ref, 0, sem).wait()
        _row_copy(hn2_ref, 0, xs_ref, 0, sem).wait()

    _drain_rows(rows, wait_pair)


def _dispatch(dest_tiles, hn2, xs0):
    n_tok, d = hn2.shape
    n_rows_out = xs0.shape[0]
    rows = MOVE_ROWS
    return pl.pallas_call(
        _dispatch_kernel,
        grid=(n_tok // rows,),
        in_specs=[pl.BlockSpec((2 * rows,), lambda i: (i,), memory_space=pltpu.SMEM),
                  pl.BlockSpec((rows, d), lambda i: (i, 0)),
                  pl.BlockSpec(memory_space=pl.ANY)],
        out_specs=pl.BlockSpec(memory_space=pl.ANY),
        out_shape=jax.ShapeDtypeStruct((n_rows_out, d), hn2.dtype),
        scratch_shapes=[pltpu.SemaphoreType.DMA],
        input_output_aliases={2: 0},
        compiler_params=pltpu.CompilerParams(dimension_semantics=("arbitrary",)),
        name="dispatch",
    )(dest_tiles, hn2, xs0)


def _expert_kernel(blk_ref, meta_ref, xs_ref, wg_ref, wu_ref, wd_ref, ys_ref, wg_s, wu_s, wd_s):
    j = pl.program_id(0)
    used = j < meta_ref[0]
    first_of_expert = (j == 0) | (blk_ref[j] != blk_ref[jnp.maximum(j - 1, 0)])

    @pl.when(used & first_of_expert)
    def _():
        wg_s[...] = wg_ref[...].astype(BF16)
        wu_s[...] = wu_ref[...].astype(BF16)
        wd_s[...] = wd_ref[...].astype(BF16)

    @pl.when(used)
    def _():
        dq = ys_ref.shape[2]
        half = xs_ref.shape[0] // 2
        halves = [slice(0, half), slice(half, 2 * half)]

        def up_proj(rows):
            xb = _unpack_bf16_pairs(xs_ref[rows, :]).astype(BF16)
            return (jnp.dot(xb, wg_s[...], preferred_element_type=F32),
                    jnp.dot(xb, wu_s[...], preferred_element_type=F32))

        def activate(gp, up):
            return (gp * _sigmoid(gp) * up).astype(BF16)

        def down_proj(rows, act):
            packed = _pack_bf16_pairs(jnp.dot(act, wd_s[...], preferred_element_type=F32))
            ys_ref[0, rows, :] = packed[:, 0:dq]
            ys_ref[1, rows, :] = packed[:, dq:2 * dq]

        gu0 = up_proj(halves[0])
        act0 = activate(*gu0)
        gu1 = up_proj(halves[1])
        down_proj(halves[0], act0)
        act1 = activate(*gu1)
        down_proj(halves[1], act1)

    @pl.when(jnp.logical_not(used))
    def _():
        ys_ref[...] = jnp.zeros_like(ys_ref)


def _experts(block_e, n_used, xs, w_gate, w_up, w_down, n_blocks):
    dp = xs.shape[1]
    ff, d = w_down.shape[1], w_down.shape[2]

    def row_block(j, be, nu):
        return (jnp.minimum(j, nu[0] - 1), 0)

    def expert_block(j, be, nu):
        return (be[jnp.minimum(j, nu[0] - 1)], 0, 0)

    grid_spec = pltpu.PrefetchScalarGridSpec(
        num_scalar_prefetch=2,
        grid=(n_blocks,),
        in_specs=[pl.BlockSpec((MOE_BLOCK, dp), row_block),
                  pl.BlockSpec((None, d, ff), expert_block),
                  pl.BlockSpec((None, d, ff), expert_block),
                  pl.BlockSpec((None, ff, d), expert_block)],
        out_specs=pl.BlockSpec((2, MOE_BLOCK, dp // 2), lambda j, be, nu: (0, j, 0)),
        scratch_shapes=[pltpu.VMEM((d, ff), BF16), pltpu.VMEM((d, ff), BF16), pltpu.VMEM((ff, d), BF16)],
    )
    return pl.pallas_call(
        _expert_kernel,
        grid_spec=grid_spec,
        out_shape=jax.ShapeDtypeStruct((2, xs.shape[0], dp // 2), xs.dtype),
        compiler_params=pltpu.CompilerParams(dimension_semantics=("arbitrary",),
                                             vmem_limit_bytes=VMEM_LIMIT),
        name="experts",
    )(block_e, n_used, xs, w_gate, w_up, w_down)


def _gather_rows_sc(table, idx):
    n, w = idx.shape[0], table.shape[1]
    mesh = plsc.VectorSubcoreMesh(core_axis_name="c", subcore_axis_name="s")

    @pl.kernel(out_type=jax.ShapeDtypeStruct((n, w), table.dtype), mesh=mesh, scratch_types=[])
    def gather(x_hbm, i_hbm, o_hbm):
        def body(i_vmem, o_vmem):
            pltpu.sync_copy(x_hbm.at[i_vmem.at[0]], o_vmem)

        pltpu.emit_pipeline(
            body,
            grid=(n // SC_WINDOW,),
            in_specs=[pl.BlockSpec((1, SC_WINDOW), lambda i: (0, i))],
            out_specs=[pl.BlockSpec((SC_WINDOW, w), lambda i: (i, 0))],
            core_axis_name=("c", "s"),
            dimension_semantics=(pltpu.PARALLEL,),
        )(i_hbm, o_hbm)

    return gather(table, idx.reshape(1, n))


def _combine_kernel(rows_ref, rw_ref, h1_ref, ada_ref, g_ref, out_ref):
    rw = rw_ref[...]

    def choice(k):
        return _unpack_bf16_pairs(jnp.concatenate([rows_ref[0, k], rows_ref[1, k]], axis=1))

    y = rw[:, 0:1] * choice(0) + rw[:, 1:2] * choice(1)
    gt2 = ada_ref[5:6, :]
    out_ref[...] = h1_ref[...] + gt2 * _rms_norm(y, g_ref[...])


def _combine(dest, ys, rw_cols, h1, ada, g_post, seq):
    n_tok, d = h1.shape
    rows = MOVE_ROWS
    per_batch = seq // rows
    n_half, n_rows, dq = ys.shape
    flat = dest.reshape(-1)
    idx = jnp.concatenate([flat + h * n_rows for h in range(n_half)])
    picked = _gather_rows_sc(ys.reshape(n_half * n_rows, dq), idx).reshape(n_half, 2, n_tok, dq)
    return pl.pallas_call(
        _combine_kernel,
        grid=(n_tok // rows,),
        in_specs=[pl.BlockSpec((n_half, 2, rows, dq), lambda i: (0, 0, i, 0)),
                  pl.BlockSpec((rows, 2), lambda i: (i, 0)),
                  pl.BlockSpec((rows, d), lambda i: (i, 0)),
                  pl.BlockSpec((None, 6, d), lambda i: (i // per_batch, 0, 0)),
                  pl.BlockSpec((1, d), lambda i: (0, 0))],
        out_specs=pl.BlockSpec((rows, d), lambda i: (i, 0)),
        out_shape=jax.ShapeDtypeStruct((n_tok, d), F32),
        compiler_params=pltpu.CompilerParams(dimension_semantics=("arbitrary",),
                                             vmem_limit_bytes=VMEM_LIMIT),
        name="combine",
    )(picked, rw_cols, h1, ada, g_post)


def kernel(x, c, w_ada, b_ada, g_pre_mix, g_post_mix, w_in, hg_lb_logits, hg_norm_g, pool_w, pool_scale,
           w_branch_hg, w_branch_pool, w_gate, b_gate, w_out, g_pre_ffn, g_post_ffn, w_router_group,
           b_router_group, w_router_expert, b_router_expert, w_exp_gate, w_exp_up, w_exp_down):
    depth = w_in.shape[0]
    bsz, seq, d = x.shape
    n_tok = bsz * seq
    n_blocks = -(-(n_tok * 2) // MOE_BLOCK) + N_EXPERTS
    assert seq % (MIX_ROWS * MIX_SUBS) == 0 and MIX_ROWS % CHUNK == 0 and seq % MOVE_ROWS == 0
    assert n_tok % ROUTE_COLS == 0 and hg_lb_logits.shape[0] == 2 and depth == 1

    h = x
    for l in range(depth):
        ada = _ada(c, w_ada[l], b_ada[l]).reshape(bsz, 6, d)
        w1 = jnp.concatenate([w_in[l], w_gate[l]], axis=1).astype(BF16)
        pad_g = jnp.zeros((d, 8 - MOE_GROUPS), F32)
        pad_e = jnp.zeros((d, ROUTER_COLS - 8 - N_EXPERTS), F32)
        w_router = jnp.concatenate([w_router_group[l], pad_g, w_router_expert[l], pad_e], axis=1)
        w_router_hi = w_router.astype(BF16)
        w_router_lo = (w_router - w_router_hi.astype(F32)).astype(BF16)
        w_router_t = jnp.concatenate([w_router_hi, w_router_lo], axis=1)
        b_router = jnp.concatenate(
            [b_router_group[l], jnp.zeros((8 - MOE_GROUPS,), F32), b_router_expert[l],
             jnp.zeros((ROUTER_COLS - 8 - N_EXPERTS,), F32)])[:, None]
        h1, hn2, eid, rw, step_counts, xs0 = _mixer(
            h, ada, g_pre_mix[l][None], g_post_mix[l][None], g_pre_ffn[l][None], w1, b_gate[l][None],
            hg_lb_logits, hg_norm_g[l][None], pool_w[l].astype(BF16), pool_scale[l][None],
            w_branch_hg[l].astype(BF16), w_branch_pool[l].astype(BF16), w_out[l].astype(BF16),
            w_router_t, b_router, n_blocks * MOE_BLOCK)

        dest, block_e, meta = _route(eid, step_counts, n_blocks)
        n_mt = n_tok // MOVE_ROWS
        dest_tiles = dest.reshape(2, n_mt, MOVE_ROWS).transpose(1, 0, 2).reshape(-1)
        xs = _dispatch(dest_tiles, hn2.reshape(n_tok, d // 2), xs0)
        ys = _experts(block_e[0, :n_blocks], meta[0, :1], xs, w_exp_gate[l], w_exp_up[l], w_exp_down[l],
                      n_blocks)
        h = _combine(dest, ys, rw.T, h1.reshape(n_tok, d), ada, g_post_ffn[l][None], seq)
        h = h.reshape(bsz, seq, d)
    return h
```

```python
import jax
import jax.numpy as jnp
from jax import lax
from jax.experimental import pallas as pl
from jax.experimental.pallas import tpu as pltpu
from jax.experimental.pallas import tpu_sc as plsc

F32 = jnp.float32
BF16 = jnp.bfloat16

CHUNK = 64
HG_HEADS = 4
POOL_WINDOWS = (2, 4, 8, 16)
POOL_GROUPS = 4
MOE_GROUPS = 4
MOE_EPG = 8
N_EXPERTS = MOE_GROUPS * MOE_EPG
EPS = 1e-6

MIX_ROWS = 256
MIX_SUBS = 2
POOL_HALO = 128
ROUTE_COLS = 512
MOE_BLOCK = 512
MOVE_ROWS = 1024
SC_WINDOW = 128
ROUTER_COLS = 128
LOG_DECAY_SPAN = 80.0
VMEM_LIMIT = 56 * 1024 * 1024

NT_DIMS = (((1,), (1,)), ((), ()))
TN_DIMS = (((0,), (0,)), ((), ()))


def _sigmoid(v):
    return 0.5 * jnp.tanh(0.5 * v) + 0.5


def _rms_norm(v, g):
    return v * lax.rsqrt(jnp.mean(v * v, axis=-1, keepdims=True) + EPS) * g


def _pack_bf16_pairs(v):
    n = v.shape[1] // 2
    lo = lax.bitcast_convert_type(v[:, :n].astype(BF16).astype(F32), jnp.uint32)
    hi = lax.bitcast_convert_type(v[:, n:].astype(BF16).astype(F32), jnp.uint32)
    return hi | (lo >> 16)


def _unpack_bf16_pairs(p):
    lo = lax.bitcast_convert_type(p << 16, F32)
    hi = lax.bitcast_convert_type(p & jnp.uint32(0xFFFF0000), F32)
    return jnp.concatenate([lo, hi], axis=1)


def _ada_kernel(c_ref, w_ref, b_ref, o_ref):
    c = c_ref[...]
    cond = c * _sigmoid(c)
    o_ref[...] = jnp.dot(cond.astype(BF16), w_ref[...].astype(BF16), preferred_element_type=F32) + b_ref[...]


def _ada(c, w_ada, b_ada):
    bsz, d = c.shape
    n_out = w_ada.shape[1]
    return pl.pallas_call(
        _ada_kernel,
        grid=(n_out // d,),
        in_specs=[pl.BlockSpec((bsz, d), lambda i: (0, 0)),
                  pl.BlockSpec((d, d), lambda i: (0, i)),
                  pl.BlockSpec((1, d), lambda i: (0, i))],
        out_specs=pl.BlockSpec((bsz, d), lambda i: (0, i)),
        out_shape=jax.ShapeDtypeStruct((bsz, n_out), F32),
        name="ada",
    )(c, w_ada, b_ada.reshape(1, n_out))


class _Sub:
    pass


def _mix_kernel(x_ref, ada_ref, gpre_ref, gpost_ref, gffn_ref, w1_ref, bgate_ref, lbl_ref, hgn_ref,
                tril_ref, band_ref, poolw_ref, pools_ref, wbh_ref, wbp_ref, wout_ref, wr_ref, br_ref,
                h1_ref, hn2_ref, eid_ref, rw_ref, cnt_ref, blank_ref,
                st_ref, ext_ref, o_scr):
    step_rows, d = x_ref.shape
    t_rows = MIX_ROWS
    hgw = hgn_ref.shape[1]
    hd = hgw // HG_HEADS
    pw = pools_ref.shape[1]
    gd = pw // POOL_GROUPS
    n_chunks = t_rows // CHUNK
    c_u = 4 * hgw
    c_gate = c_u + pw
    j = pl.program_id(1)

    @pl.when(j == 0)
    def _():
        st_ref[...] = jnp.zeros_like(st_ref)
        ext_ref[step_rows:step_rows + POOL_HALO, :] = jnp.zeros((POOL_HALO, pw), BF16)

    ext_ref[0:POOL_HALO, :] = ext_ref[step_rows:step_rows + POOL_HALO, :]

    blank_ref[...] = jnp.zeros(blank_ref.shape, blank_ref.dtype)

    ada = ada_ref[...]
    sh1, sc1, gt1 = ada[0:1], ada[1:2], ada[2:3]
    sh2, sc2 = ada[3:4], ada[4:5]
    lbl = lbl_ref[...]
    lmax = jnp.maximum(lbl[0:1], lbl[1:2])
    e0 = jnp.exp(lbl[0:1] - lmax)
    lb = e0 / (e0 + jnp.exp(lbl[1:2] - lmax))
    row = lax.broadcasted_iota(jnp.int32, (CHUNK, CHUNK), 0)
    col = lax.broadcasted_iota(jnp.int32, (CHUNK, CHUNK), 1)
    causal = row >= col

    def chunk_heads():
        for ci in range(n_chunks):
            for h in range(HG_HEADS):
                yield ci, h, slice(ci * CHUNK, (ci + 1) * CHUNK), slice(h * hd, (h + 1) * hd)

    def m_project(c):
        x = x_ref[c.rows, :]
        hn = _rms_norm(x, gpre_ref[...]) * (1.0 + sc1) + sh1
        c.proj = jnp.dot(hn.astype(BF16), w1_ref[...], preferred_element_type=F32)

    def e_gates(c):
        qr, fr = c.proj[:, 0:hgw], c.proj[:, hgw:2 * hgw]
        vr, gr = c.proj[:, 2 * hgw:3 * hgw], c.proj[:, 3 * hgw:4 * hgw]
        c.q = qr * _sigmoid(qr)
        f = lb + (1.0 - lb) * _sigmoid(fr)
        c.k = 1.0 - f
        lf = jnp.log(f)
        c.lf_hi = lf.astype(BF16)
        c.lf_lo = (lf - c.lf_hi.astype(F32)).astype(BF16)
        c.vb = vr.astype(BF16)
        c.og = hgn_ref[...] * (gr * _sigmoid(gr))
        c.u = c.proj[:, c_u:c_gate]
        ext_ref[c.ext0 + POOL_HALO:c.ext0 + POOL_HALO + t_rows, :] = c.u.astype(BF16)

    def m_cumsum(c):
        tril = tril_ref[...]
        c.b = (jnp.dot(tril, c.lf_hi, preferred_element_type=F32)
               + jnp.dot(tril, c.lf_lo, preferred_element_type=F32))

    def e_decays(c):
        c.qt, c.kt, c.kh, c.dec, c.mid = [], [], [], [], []
        for ci in range(n_chunks):
            rs = slice(ci * CHUNK, (ci + 1) * CHUNK)
            b_c = c.b[rs, :]
            b_mid = b_c[CHUNK // 2 - 1:CHUNK // 2, :]
            b_last = b_c[CHUNK - 1:CHUNK, :]
            rel = jnp.clip(b_c - b_mid, -LOG_DECAY_SPAN, LOG_DECAY_SPAN)
            c.qt.append((c.q[rs, :] * jnp.exp(rel)).astype(BF16))
            c.kt.append((c.k[rs, :] * jnp.exp(-rel)).astype(BF16))
            c.kh.append((c.k[rs, :] * jnp.exp(b_last - b_c)).astype(BF16))
            c.dec.append(jnp.exp(b_last))
            c.mid.append(jnp.exp(b_mid))

    def m_scores(c):
        c.s, c.upd = {}, {}
        for ci, h, rs, cs in chunk_heads():
            c.s[ci, h] = lax.dot_general(c.qt[ci][:, cs], c.kt[ci][:, cs], NT_DIMS, preferred_element_type=F32)
            c.upd[ci, h] = lax.dot_general(c.vb[rs, cs], c.kh[ci][:, cs], TN_DIMS,
                                           preferred_element_type=F32)

    def e_states(c):
        c.sb, c.stb = {}, {}
        for ci, h, rs, cs in chunk_heads():
            c.sb[ci, h] = jnp.where(causal, c.s[ci, h], 0.0).astype(BF16)
        for h in range(HG_HEADS):
            cs = slice(h * hd, (h + 1) * hd)
            st = st_ref[h]
            for ci in range(n_chunks):
                c.stb[ci, h] = (c.mid[ci][:, cs] * st).astype(BF16)
                st = c.dec[ci][:, cs] * st + c.upd[ci, h]
            st_ref[h] = st

    def m_outputs(c):
        c.o = {}
        for ci, h, rs, cs in chunk_heads():
            c.o[ci, h] = (jnp.dot(c.sb[ci, h], c.vb[rs, cs], preferred_element_type=F32)
                          + lax.dot_general(c.qt[ci][:, cs], c.stb[ci, h], NT_DIMS,
                                            preferred_element_type=F32))

    def e_head_norm(c):
        for ci, h, rs, cs in chunk_heads():
            o = c.o[ci, h]
            o = o * lax.rsqrt(jnp.mean(o * o, axis=-1, keepdims=True) + EPS)
            o_scr[c.row0 + ci * CHUNK:c.row0 + (ci + 1) * CHUNK, cs] = (o * c.og[rs, cs]).astype(BF16)

    def m_branch_hg(c):
        c.y_hg = jnp.dot(o_scr[c.rows, :], wbh_ref[...], preferred_element_type=F32)
        ext = ext_ref[c.ext0:c.ext0 + POOL_HALO + t_rows, :]
        c.wsum = [jnp.dot(band_ref[g], ext[:, g * gd:(g + 1) * gd], preferred_element_type=F32)
                  for g in range(POOL_GROUPS)]

    def e_pooled(c):
        pos = j * step_rows + c.row0 + lax.broadcasted_iota(jnp.int32, (t_rows, gd), 0)
        c.pooled = []
        for g in range(POOL_GROUPS):
            cnt = jnp.minimum(pos + 1, POOL_WINDOWS[g]).astype(F32)
            c.pooled.append((c.wsum[g] / cnt - c.u[:, g * gd:(g + 1) * gd]).astype(BF16))

    def m_pool_mix(c):
        c.mixed = [jnp.dot(c.pooled[g], poolw_ref[g], preferred_element_type=F32) for g in range(POOL_GROUPS)]

    def e_gate(c):
        c.mixed = (jnp.concatenate(c.mixed, axis=1) * pools_ref[...]).astype(BF16)
        c.gate = _sigmoid(c.proj[:, c_gate:] + bgate_ref[...])

    def m_branch_pool(c):
        c.y_pool = jnp.dot(c.mixed, wbp_ref[...], preferred_element_type=F32)

    def e_merge(c):
        c.merged = (c.gate[:, 0:d] * c.y_hg + c.gate[:, d:2 * d] * c.y_pool).astype(BF16)

    def m_out(c):
        c.mix = jnp.dot(c.merged, wout_ref[...], preferred_element_type=F32)

    def e_residual(c):
        h1 = x_ref[c.rows, :] + gt1 * _rms_norm(c.mix, gpost_ref[...])
        h1_ref[c.rows, :] = h1
        hn2 = _rms_norm(h1, gffn_ref[...]) * (1.0 + sc2) + sh2
        packed = _pack_bf16_pairs(hn2)
        dq = hn2_ref.shape[2]
        hn2_ref[0, c.rows, :] = packed[:, 0:dq]
        hn2_ref[1, c.rows, :] = packed[:, dq:2 * dq]
        hn2_hi = hn2.astype(BF16)
        c.hn2_split = jnp.concatenate([hn2_hi, (hn2 - hn2_hi.astype(F32)).astype(BF16)], axis=0)

    def m_router(c):
        c.prod = jnp.dot(c.hn2_split, wr_ref[...], preferred_element_type=F32)

    def e_route(c):
        n_r = wr_ref.shape[1] // 2
        logits = (c.prod[0:t_rows, 0:n_r] + c.prod[t_rows:2 * t_rows, 0:n_r]
                  + c.prod[0:t_rows, n_r:2 * n_r])
        lt = logits.T + br_ref[...]
        lg = lt[0:MOE_GROUPS]
        gmax = jnp.max(lg, axis=0, keepdims=True)
        p_g = 1.0 / jnp.sum(jnp.exp(lg - gmax), axis=0, keepdims=True)
        gi = lax.broadcasted_iota(jnp.int32, lg.shape, 0).astype(F32)
        g_idx = jnp.min(jnp.where(lg == gmax, gi, float(MOE_GROUPS)), axis=0, keepdims=True)
        le = lt[8:8 + MOE_EPG]
        for g in range(1, MOE_GROUPS):
            le = jnp.where(g_idx == float(g), lt[8 + g * MOE_EPG:8 + (g + 1) * MOE_EPG], le)
        ei = lax.broadcasted_iota(jnp.int32, le.shape, 0).astype(F32)
        m1 = jnp.max(le, axis=0, keepdims=True)
        i1 = jnp.min(jnp.where(le == m1, ei, float(MOE_EPG)), axis=0, keepdims=True)
        le2 = jnp.where(ei == i1, -jnp.inf, le)
        m2 = jnp.max(le2, axis=0, keepdims=True)
        i2 = jnp.min(jnp.where(le2 == m2, ei, float(MOE_EPG)), axis=0, keepdims=True)
        r = jnp.exp(m2 - m1)
        w_first = p_g / (1.0 + r)
        e_first, e_second = g_idx * MOE_EPG + i1, g_idx * MOE_EPG + i2
        eid_ref[0:1, c.rows] = e_first.astype(jnp.int32)
        eid_ref[1:2, c.rows] = e_second.astype(jnp.int32)
        ex = lax.broadcasted_iota(jnp.int32, (N_EXPERTS, t_rows), 0).astype(F32)
        hits = jnp.where((ex == e_first) | (ex == e_second), 1.0, 0.0)
        cnt = jnp.broadcast_to(jnp.sum(hits, axis=1, keepdims=True), cnt_ref.shape)
        cnt_ref[...] = cnt if c.row0 == 0 else cnt_ref[...] + cnt
        rw_ref[0:1, c.rows] = w_first
        rw_ref[1:2, c.rows] = w_first * r

    stages = [m_project, e_gates, m_cumsum, e_decays, m_scores, e_states, m_outputs, e_head_norm,
              m_branch_hg, e_pooled, m_pool_mix, e_gate, m_branch_pool, e_merge, m_out, e_residual,
              m_router, e_route]

    subs = []
    for k in range(step_rows // t_rows):
        c = _Sub()
        c.row0 = k * t_rows
        c.rows = slice(c.row0, c.row0 + t_rows)
        c.ext0 = k * t_rows
        subs.append(c)
    for t in range(len(stages) + len(subs) - 1):
        for k, c in enumerate(subs):
            if 0 <= t - k < len(stages):
                stages[t - k](c)


def _band_matrices(t_rows):
    t = jnp.arange(t_rows)[:, None] + POOL_HALO
    jx = jnp.arange(t_rows + POOL_HALO)[None, :]
    return jnp.stack([((jx <= t) & (jx > t - w)) for w in POOL_WINDOWS]).astype(BF16)


def _chunk_tril(t_rows):
    r = jnp.arange(t_rows)[:, None]
    c = jnp.arange(t_rows)[None, :]
    return ((r >= c) & (r // CHUNK == c // CHUNK)).astype(BF16)


def _mixer(x, ada, g_pre, g_post, g_ffn, w1, b_gate, lb_logits, hg_norm_g, pool_w, pool_scale,
           w_bh, w_bp, w_out, w_router_t, b_router, blank_rows):
    bsz, seq, d = x.shape
    t_rows = MIX_ROWS
    step_rows = MIX_ROWS * MIX_SUBS
    n_t = seq // step_rows
    n_tok = bsz * seq
    hgw = hg_norm_g.shape[1]
    pw = pool_scale.shape[1]
    n_r = b_router.shape[0]

    def const(shape):
        return pl.BlockSpec(shape, lambda b, j: (0,) * len(shape), pipeline_mode=pl.Buffered(1))

    in_specs = [
        pl.BlockSpec((None, step_rows, d), lambda b, j: (b, j, 0)),
        pl.BlockSpec((None, 6, d), lambda b, j: (b, 0, 0)),
        const((1, d)), const((1, d)), const((1, d)),
        const(w1.shape), const((1, 2 * d)),
        const(lb_logits.shape), const((1, hgw)),
        const((t_rows, t_rows)), const((POOL_GROUPS, t_rows, t_rows + POOL_HALO)),
        const(pool_w.shape), const((1, pw)),
        const(w_bh.shape), const(w_bp.shape), const(w_out.shape),
        const(w_router_t.shape), const((n_r, 1)),
    ]
    out_specs = [
        pl.BlockSpec((None, step_rows, d), lambda b, j: (b, j, 0)),
        pl.BlockSpec((2, None, step_rows, d // 4), lambda b, j: (0, b, j, 0)),
        pl.BlockSpec((2, step_rows), lambda b, j: (0, b * n_t + j)),
        pl.BlockSpec((2, step_rows), lambda b, j: (0, b * n_t + j)),
        pl.BlockSpec((None, N_EXPERTS, 128), lambda b, j: (b * n_t + j, 0, 0)),
        pl.BlockSpec((blank_rows // (bsz * n_t), d // 2), lambda b, j: (b * n_t + j, 0)),
    ]
    out_shape = [
        jax.ShapeDtypeStruct((bsz, seq, d), F32),
        jax.ShapeDtypeStruct((2, bsz, seq, d // 4), jnp.uint32),
        jax.ShapeDtypeStruct((2, n_tok), jnp.int32),
        jax.ShapeDtypeStruct((2, n_tok), F32),
        jax.ShapeDtypeStruct((bsz * n_t, N_EXPERTS, 128), F32),
        jax.ShapeDtypeStruct((blank_rows, d // 2), jnp.uint32),
    ]
    assert blank_rows % (bsz * n_t * 8) == 0
    scratch = [
        pltpu.VMEM((HG_HEADS, hgw // HG_HEADS, hgw // HG_HEADS), F32),
        pltpu.VMEM((step_rows + POOL_HALO, pw), BF16),
        pltpu.VMEM((step_rows, hgw), BF16),
    ]
    return pl.pallas_call(
        _mix_kernel,
        grid=(bsz, n_t),
        in_specs=in_specs, out_specs=out_specs, out_shape=out_shape, scratch_shapes=scratch,
        compiler_params=pltpu.CompilerParams(dimension_semantics=("arbitrary", "arbitrary"),
                                             vmem_limit_bytes=VMEM_LIMIT),
        name="mixer",
    )(x, ada, g_pre, g_post, g_ffn, w1, b_gate, lb_logits, hg_norm_g, _chunk_tril(t_rows),
      _band_matrices(t_rows), pool_w, pool_scale, w_bh, w_bp, w_out, w_router_t, b_router)


def _route_kernel(eid_ref, cnt_ref, dest_ref, blk_ref, meta_ref, run_ref):
    i = pl.program_id(0)
    cols = eid_ref.shape[1]
    eidx = lax.broadcasted_iota(jnp.int32, (N_EXPERTS, cols), 0)
    hot0 = eidx == eid_ref[0:1, :]
    hot1 = eidx == eid_ref[1:2, :]
    both = jnp.where(hot0 | hot1, 1.0, 0.0)
    tile_cnt = jnp.sum(both, axis=1, keepdims=True)

    @pl.when(i == 0)
    def _():
        total = jnp.sum(cnt_ref[...], axis=0)
        nblk_f = jnp.floor((total + float(MOE_BLOCK - 1)) * (1.0 / MOE_BLOCK))
        er = lax.broadcasted_iota(jnp.int32, (N_EXPERTS, N_EXPERTS), 0)
        ec = lax.broadcasted_iota(jnp.int32, (N_EXPERTS, N_EXPERTS), 1)
        lower = jnp.where(ec < er, 1.0, 0.0).astype(BF16)
        start_blk = jnp.dot(lower, nblk_f.astype(BF16), preferred_element_type=F32)
        run_ref[...] = start_blk[:, 0:1] * float(MOE_BLOCK)
        end_blk = start_blk + nblk_f
        lane = lax.broadcasted_iota(jnp.int32, (N_EXPERTS, blk_ref.shape[1]), 1).astype(F32)
        done = jnp.where(end_blk[:, 0:1] <= lane, 1.0, 0.0)
        blk_ref[...] = jnp.minimum(jnp.sum(done, axis=0, keepdims=True),
                                   float(N_EXPERTS - 1)).astype(jnp.int32)
        meta_ref[...] = jnp.broadcast_to(end_blk[N_EXPERTS - 1:N_EXPERTS, 0:1],
                                         meta_ref.shape).astype(jnp.int32)

    r = lax.broadcasted_iota(jnp.int32, (cols, cols), 0)
    c = lax.broadcasted_iota(jnp.int32, (cols, cols), 1)
    before = jnp.where(r < c, 1.0, 0.0).astype(BF16)
    prefix = jnp.dot(both.astype(BF16), before, preferred_element_type=F32)
    slot = run_ref[...] + prefix
    dest_ref[0:1, :] = jnp.sum(jnp.where(hot0, slot, 0.0), axis=0, keepdims=True).astype(jnp.int32)
    dest_ref[1:2, :] = jnp.sum(jnp.where(hot1, slot, 0.0), axis=0, keepdims=True).astype(jnp.int32)
    run_ref[...] += tile_cnt


def _route(eid, step_counts, n_blocks):
    n_tok = eid.shape[1]
    cols = ROUTE_COLS
    blk_lanes = pl.cdiv(n_blocks, 128) * 128
    return pl.pallas_call(
        _route_kernel,
        grid=(n_tok // cols,),
        in_specs=[pl.BlockSpec((2, cols), lambda i: (0, i)),
                  pl.BlockSpec(step_counts.shape, lambda i: (0, 0, 0))],
        out_specs=[pl.BlockSpec((2, cols), lambda i: (0, i)),
                   pl.BlockSpec((1, blk_lanes), lambda i: (0, 0)),
                   pl.BlockSpec((1, 128), lambda i: (0, 0))],
        out_shape=[jax.ShapeDtypeStruct((2, n_tok), jnp.int32),
                   jax.ShapeDtypeStruct((1, blk_lanes), jnp.int32),
                   jax.ShapeDtypeStruct((1, 128), jnp.int32)],
        scratch_shapes=[pltpu.VMEM((N_EXPERTS, 1), F32)],
        compiler_params=pltpu.CompilerParams(dimension_semantics=("arbitrary",)),
        name="route",
    )(eid, step_counts)


def _expert_kernel(blk_ref, meta_ref, xs_ref, wg_ref, wu_ref, wd_ref, ys_ref, wg_s, wu_s, wd_s):
    j = pl.program_id(0)
    used = j < meta_ref[0]
    first_of_expert = (j == 0) | (blk_ref[j] != blk_ref[jnp.maximum(j - 1, 0)])

    @pl.when(used & first_of_expert)
    def _():
        wg_s[...] = wg_ref[...].astype(BF16)
        wu_s[...] = wu_ref[...].astype(BF16)
        wd_s[...] = wd_ref[...].astype(BF16)

    @pl.when(used)
    def _():
        dq = ys_ref.shape[2]
        half = xs_ref.shape[1] // 2
        halves = [slice(0, half), slice(half, 2 * half)]

        def up_proj(rows):
            xb = _unpack_bf16_pairs(jnp.concatenate([xs_ref[0, rows, :], xs_ref[1, rows, :]], axis=1)).astype(BF16)
            return (jnp.dot(xb, wg_s[...], preferred_element_type=F32),
                    jnp.dot(xb, wu_s[...], preferred_element_type=F32))

        def activate(gp, up):
            return (gp * _sigmoid(gp) * up).astype(BF16)

        def down_proj(rows, act):
            packed = _pack_bf16_pairs(jnp.dot(act, wd_s[...], preferred_element_type=F32))
            ys_ref[0, rows, :] = packed[:, 0:dq]
            ys_ref[1, rows, :] = packed[:, dq:2 * dq]

        gu0 = up_proj(halves[0])
        act0 = activate(*gu0)
        gu1 = up_proj(halves[1])
        down_proj(halves[0], act0)
        act1 = activate(*gu1)
        down_proj(halves[1], act1)

    @pl.when(jnp.logical_not(used))
    def _():
        ys_ref[...] = jnp.zeros_like(ys_ref)


def _experts(block_e, n_used, xs, w_gate, w_up, w_down, n_blocks):
    dp = 2 * xs.shape[2]
    ff, d = w_down.shape[1], w_down.shape[2]

    def row_block(j, be, nu):
        return (0, jnp.minimum(j, nu[0] - 1), 0)

    def expert_block(j, be, nu):
        return (be[jnp.minimum(j, nu[0] - 1)], 0, 0)

    grid_spec = pltpu.PrefetchScalarGridSpec(
        num_scalar_prefetch=2,
        grid=(n_blocks,),
        in_specs=[pl.BlockSpec((2, MOE_BLOCK, dp // 2), row_block),
                  pl.BlockSpec((None, d, ff), expert_block),
                  pl.BlockSpec((None, d, ff), expert_block),
                  pl.BlockSpec((None, ff, d), expert_block)],
        out_specs=pl.BlockSpec((2, MOE_BLOCK, dp // 2), lambda j, be, nu: (0, j, 0)),
        scratch_shapes=[pltpu.VMEM((d, ff), BF16), pltpu.VMEM((d, ff), BF16), pltpu.VMEM((ff, d), BF16)],
    )
    return pl.pallas_call(
        _expert_kernel,
        grid_spec=grid_spec,
        out_shape=jax.ShapeDtypeStruct(xs.shape, xs.dtype),
        compiler_params=pltpu.CompilerParams(dimension_semantics=("arbitrary",),
                                             vmem_limit_bytes=VMEM_LIMIT),
        name="experts",
    )(block_e, n_used, xs, w_gate, w_up, w_down)


def _gather_rows_sc(table, idx):
    n, w = idx.shape[0], table.shape[1]
    mesh = plsc.VectorSubcoreMesh(core_axis_name="c", subcore_axis_name="s")

    @pl.kernel(out_type=jax.ShapeDtypeStruct((n, w), table.dtype), mesh=mesh, scratch_types=[])
    def gather(x_hbm, i_hbm, o_hbm):
        def body(i_vmem, o_vmem):
            pltpu.sync_copy(x_hbm.at[i_vmem.at[0]], o_vmem)

        pltpu.emit_pipeline(
            body,
            grid=(n // SC_WINDOW,),
            in_specs=[pl.BlockSpec((1, SC_WINDOW), lambda i: (0, i))],
            out_specs=[pl.BlockSpec((SC_WINDOW, w), lambda i: (i, 0))],
            core_axis_name=("c", "s"),
            dimension_semantics=(pltpu.PARALLEL,),
        )(i_hbm, o_hbm)

    return gather(table, idx.reshape(1, n))


def _scatter_rows_sc(src, idx, n_out):
    n, w = idx.shape[0], src.shape[1]
    n_src_windows = src.shape[0] // SC_WINDOW
    mesh = plsc.VectorSubcoreMesh(core_axis_name="c", subcore_axis_name="s")

    @pl.kernel(out_type=jax.ShapeDtypeStruct((n_out, w), src.dtype), mesh=mesh, scratch_types=[])
    def scatter(x_hbm, i_hbm, o_hbm):
        def body(x_vmem, i_vmem):
            pltpu.sync_copy(x_vmem, o_hbm.at[i_vmem.at[0]])

        pltpu.emit_pipeline(
            body,
            grid=(n // SC_WINDOW,),
            in_specs=[pl.BlockSpec((SC_WINDOW, w), lambda i: (i % n_src_windows, 0)),
                      pl.BlockSpec((1, SC_WINDOW), lambda i: (0, i))],
            out_specs=[],
            core_axis_name=("c", "s"),
            dimension_semantics=(pltpu.PARALLEL,),
        )(x_hbm, i_hbm)

    return scatter(src, idx.reshape(1, n))


def _combine_kernel(rows_ref, rw_ref, h1_ref, ada_ref, g_ref, out_ref):
    rw = rw_ref[...]

    def choice(k):
        return _unpack_bf16_pairs(jnp.concatenate([rows_ref[0, k], rows_ref[1, k]], axis=1))

    y = rw[:, 0:1] * choice(0) + rw[:, 1:2] * choice(1)
    gt2 = ada_ref[5:6, :]
    out_ref[...] = h1_ref[...] + gt2 * _rms_norm(y, g_ref[...])


def _combine(dest, ys, rw_cols, h1, ada, g_post, seq):
    n_tok, d = h1.shape
    rows = MOVE_ROWS
    per_batch = seq // rows
    n_half, n_rows, dq = ys.shape
    flat = dest.reshape(-1)
    idx = jnp.concatenate([flat + h * n_rows for h in range(n_half)])
    picked = _gather_rows_sc(ys.reshape(n_half * n_rows, dq), idx).reshape(n_half, 2, n_tok, dq)
    return pl.pallas_call(
        _combine_kernel,
        grid=(n_tok // rows,),
        in_specs=[pl.BlockSpec((n_half, 2, rows, dq), lambda i: (0, 0, i, 0)),
                  pl.BlockSpec((rows, 2), lambda i: (i, 0)),
                  pl.BlockSpec((rows, d), lambda i: (i, 0)),
                  pl.BlockSpec((None, 6, d), lambda i: (i // per_batch, 0, 0)),
                  pl.BlockSpec((1, d), lambda i: (0, 0))],
        out_specs=pl.BlockSpec((rows, d), lambda i: (i, 0)),
        out_shape=jax.ShapeDtypeStruct((n_tok, d), F32),
        compiler_params=pltpu.CompilerParams(dimension_semantics=("arbitrary",),
                                             vmem_limit_bytes=VMEM_LIMIT),
        name="combine",
    )(picked, rw_cols, h1, ada, g_post)


def kernel(x, c, w_ada, b_ada, g_pre_mix, g_post_mix, w_in, hg_lb_logits, hg_norm_g, pool_w, pool_scale,
           w_branch_hg, w_branch_pool, w_gate, b_gate, w_out, g_pre_ffn, g_post_ffn, w_router_group,
           b_router_group, w_router_expert, b_router_expert, w_exp_gate, w_exp_up, w_exp_down):
    depth = w_in.shape[0]
    bsz, seq, d = x.shape
    n_tok = bsz * seq
    n_blocks = -(-(n_tok * 2) // MOE_BLOCK) + N_EXPERTS
    assert seq % (MIX_ROWS * MIX_SUBS) == 0 and MIX_ROWS % CHUNK == 0 and seq % MOVE_ROWS == 0
    assert n_tok % ROUTE_COLS == 0 and hg_lb_logits.shape[0] == 2 and depth == 1

    h = x
    for l in range(depth):
        ada = _ada(c, w_ada[l], b_ada[l]).reshape(bsz, 6, d)
        w1 = jnp.concatenate([w_in[l], w_gate[l]], axis=1).astype(BF16)
        pad_g = jnp.zeros((d, 8 - MOE_GROUPS), F32)
        pad_e = jnp.zeros((d, ROUTER_COLS - 8 - N_EXPERTS), F32)
        w_router = jnp.concatenate([w_router_group[l], pad_g, w_router_expert[l], pad_e], axis=1)
        w_router_hi = w_router.astype(BF16)
        w_router_lo = (w_router - w_router_hi.astype(F32)).astype(BF16)
        w_router_t = jnp.concatenate([w_router_hi, w_router_lo], axis=1)
        b_router = jnp.concatenate(
            [b_router_group[l], jnp.zeros((8 - MOE_GROUPS,), F32), b_router_expert[l],
             jnp.zeros((ROUTER_COLS - 8 - N_EXPERTS,), F32)])[:, None]
        h1, hn2, eid, rw, step_counts, xs0 = _mixer(
            h, ada, g_pre_mix[l][None], g_post_mix[l][None], g_pre_ffn[l][None], w1, b_gate[l][None],
            hg_lb_logits, hg_norm_g[l][None], pool_w[l].astype(BF16), pool_scale[l][None],
            w_branch_hg[l].astype(BF16), w_branch_pool[l].astype(BF16), w_out[l].astype(BF16),
            w_router_t, b_router, n_blocks * MOE_BLOCK)

        dest, block_e, meta = _route(eid, step_counts, n_blocks)
        p_rows = n_blocks * MOE_BLOCK
        idx = jnp.concatenate([dest[k] + h * p_rows for k in range(2) for h in range(2)])
        xs = _scatter_rows_sc(hn2.reshape(2 * n_tok, d // 4), idx, 2 * p_rows).reshape(2, p_rows, d // 4)
        ys = _experts(block_e[0, :n_blocks], meta[0, :1], xs, w_exp_gate[l], w_exp_up[l], w_exp_down[l],
                      n_blocks)
        h = _combine(dest, ys, rw.T, h1.reshape(n_tok, d), ada, g_post_ffn[l][None], seq)
        h = h.reshape(bsz, seq, d)
    return h
```

```python
import jax
import jax.numpy as jnp
from jax import lax
from jax.experimental import pallas as pl
from jax.experimental.pallas import tpu as pltpu
from jax.experimental.pallas import tpu_sc as plsc

F32 = jnp.float32
BF16 = jnp.bfloat16

CHUNK = 64
HG_HEADS = 4
POOL_WINDOWS = (2, 4, 8, 16)
POOL_GROUPS = 4
MOE_GROUPS = 4
MOE_EPG = 8
N_EXPERTS = MOE_GROUPS * MOE_EPG
EPS = 1e-6

MIX_ROWS = 256
MIX_SUBS = 2
POOL_HALO = 128
ROUTE_COLS = 512
MOE_BLOCK = 512
MOVE_ROWS = 1024
SC_WINDOW = 128
ROUTER_COLS = 128
LOG_DECAY_SPAN = 80.0
VMEM_LIMIT = 56 * 1024 * 1024

NT_DIMS = (((1,), (1,)), ((), ()))
TN_DIMS = (((0,), (0,)), ((), ()))


def _sigmoid(v):
    return 0.5 * jnp.tanh(0.5 * v) + 0.5


def _rms_norm(v, g):
    return v * lax.rsqrt(jnp.mean(v * v, axis=-1, keepdims=True) + EPS) * g


def _pack_bf16_pairs(v):
    n = v.shape[1] // 2
    lo = lax.bitcast_convert_type(v[:, :n].astype(BF16).astype(F32), jnp.uint32)
    hi = lax.bitcast_convert_type(v[:, n:].astype(BF16).astype(F32), jnp.uint32)
    return hi | (lo >> 16)


def _unpack_bf16_pairs(p):
    lo = lax.bitcast_convert_type(p << 16, F32)
    hi = lax.bitcast_convert_type(p & jnp.uint32(0xFFFF0000), F32)
    return jnp.concatenate([lo, hi], axis=1)


def _ada_kernel(c_ref, w_ref, b_ref, o_ref):
    c = c_ref[...]
    cond = c * _sigmoid(c)
    o_ref[...] = jnp.dot(cond.astype(BF16), w_ref[...].astype(BF16), preferred_element_type=F32) + b_ref[...]


def _ada(c, w_ada, b_ada):
    bsz, d = c.shape
    n_out = w_ada.shape[1]
    return pl.pallas_call(
        _ada_kernel,
        grid=(n_out // d,),
        in_specs=[pl.BlockSpec((bsz, d), lambda i: (0, 0)),
                  pl.BlockSpec((d, d), lambda i: (0, i)),
                  pl.BlockSpec((1, d), lambda i: (0, i))],
        out_specs=pl.BlockSpec((bsz, d), lambda i: (0, i)),
        out_shape=jax.ShapeDtypeStruct((bsz, n_out), F32),
        name="ada",
    )(c, w_ada, b_ada.reshape(1, n_out))


class _Sub:
    pass


def _mix_kernel(x_ref, ada_ref, gpre_ref, gpost_ref, gffn_ref, w1_ref, bgate_ref, lbl_ref, hgn_ref,
                tril_ref, band_ref, poolw_ref, pools_ref, wbh_ref, wbp_ref, wout_ref, wr_ref, br_ref,
                h1_ref, hn2_ref, eid_ref, rw_ref, cnt_ref,
                st_ref, ext_ref, o_scr):
    step_rows, d = x_ref.shape
    t_rows = MIX_ROWS
    hgw = hgn_ref.shape[1]
    hd = hgw // HG_HEADS
    pw = pools_ref.shape[1]
    gd = pw // POOL_GROUPS
    n_chunks = t_rows // CHUNK
    c_u = 4 * hgw
    c_gate = c_u + pw
    j = pl.program_id(1)

    @pl.when(j == 0)
    def _():
        st_ref[...] = jnp.zeros_like(st_ref)
        ext_ref[step_rows:step_rows + POOL_HALO, :] = jnp.zeros((POOL_HALO, pw), BF16)

    ext_ref[0:POOL_HALO, :] = ext_ref[step_rows:step_rows + POOL_HALO, :]

    ada = ada_ref[...]
    sh1, sc1, gt1 = ada[0:1], ada[1:2], ada[2:3]
    sh2, sc2 = ada[3:4], ada[4:5]
    lbl = lbl_ref[...]
    lmax = jnp.maximum(lbl[0:1], lbl[1:2])
    e0 = jnp.exp(lbl[0:1] - lmax)
    lb = e0 / (e0 + jnp.exp(lbl[1:2] - lmax))
    row = lax.broadcasted_iota(jnp.int32, (CHUNK, CHUNK), 0)
    col = lax.broadcasted_iota(jnp.int32, (CHUNK, CHUNK), 1)
    causal = row >= col

    def chunk_heads():
        for ci in range(n_chunks):
            for h in range(HG_HEADS):
                yield ci, h, slice(ci * CHUNK, (ci + 1) * CHUNK), slice(h * hd, (h + 1) * hd)

    def m_project(c):
        x = x_ref[c.rows, :]
        hn = _rms_norm(x, gpre_ref[...]) * (1.0 + sc1) + sh1
        c.proj = jnp.dot(hn.astype(BF16), w1_ref[...], preferred_element_type=F32)

    def e_gates(c):
        qr, fr = c.proj[:, 0:hgw], c.proj[:, hgw:2 * hgw]
        vr, gr = c.proj[:, 2 * hgw:3 * hgw], c.proj[:, 3 * hgw:4 * hgw]
        c.q = qr * _sigmoid(qr)
        f = lb + (1.0 - lb) * _sigmoid(fr)
        c.k = 1.0 - f
        lf = jnp.log(f)
        c.lf_hi = lf.astype(BF16)
        c.lf_lo = (lf - c.lf_hi.astype(F32)).astype(BF16)
        c.vb = vr.astype(BF16)
        c.og = hgn_ref[...] * (gr * _sigmoid(gr))
        c.u = c.proj[:, c_u:c_gate]
        ext_ref[c.ext0 + POOL_HALO:c.ext0 + POOL_HALO + t_rows, :] = c.u.astype(BF16)

    def m_cumsum(c):
        tril = tril_ref[...]
        c.b = (jnp.dot(tril, c.lf_hi, preferred_element_type=F32)
               + jnp.dot(tril, c.lf_lo, preferred_element_type=F32))

    def e_decays(c):
        c.qt, c.kt, c.kh, c.dec, c.mid = [], [], [], [], []
        for ci in range(n_chunks):
            rs = slice(ci * CHUNK, (ci + 1) * CHUNK)
            b_c = c.b[rs, :]
            b_mid = b_c[CHUNK // 2 - 1:CHUNK // 2, :]
            b_last = b_c[CHUNK - 1:CHUNK, :]
            rel = jnp.clip(b_c - b_mid, -LOG_DECAY_SPAN, LOG_DECAY_SPAN)
            c.qt.append((c.q[rs, :] * jnp.exp(rel)).astype(BF16))
            c.kt.append((c.k[rs, :] * jnp.exp(-rel)).astype(BF16))
            c.kh.append((c.k[rs, :] * jnp.exp(b_last - b_c)).astype(BF16))
            c.dec.append(jnp.exp(b_last))
            c.mid.append(jnp.exp(b_mid))

    def m_scores(c):
        c.s, c.upd = {}, {}
        for ci, h, rs, cs in chunk_heads():
            c.s[ci, h] = lax.dot_general(c.qt[ci][:, cs], c.kt[ci][:, cs], NT_DIMS, preferred_element_type=F32)
            c.upd[ci, h] = lax.dot_general(c.vb[rs, cs], c.kh[ci][:, cs], TN_DIMS,
                                           preferred_element_type=F32)

    def e_states(c):
        c.sb, c.stb = {}, {}
        for ci, h, rs, cs in chunk_heads():
            c.sb[ci, h] = jnp.where(causal, c.s[ci, h], 0.0).astype(BF16)
        for h in range(HG_HEADS):
            cs = slice(h * hd, (h + 1) * hd)
            st = st_ref[h]
            for ci in range(n_chunks):
                c.stb[ci, h] = (c.mid[ci][:, cs] * st).astype(BF16)
                st = c.dec[ci][:, cs] * st + c.upd[ci, h]
            st_ref[h] = st

    def m_outputs(c):
        c.o = {}
        for ci, h, rs, cs in chunk_heads():
            c.o[ci, h] = (jnp.dot(c.sb[ci, h], c.vb[rs, cs], preferred_element_type=F32)
                          + lax.dot_general(c.qt[ci][:, cs], c.stb[ci, h], NT_DIMS,
                                            preferred_element_type=F32))

    def e_head_norm(c):
        for ci, h, rs, cs in chunk_heads():
            o = c.o[ci, h]
            o = o * lax.rsqrt(jnp.mean(o * o, axis=-1, keepdims=True) + EPS)
            o_scr[c.row0 + ci * CHUNK:c.row0 + (ci + 1) * CHUNK, cs] = (o * c.og[rs, cs]).astype(BF16)

    def m_branch_hg(c):
        c.y_hg = jnp.dot(o_scr[c.rows, :], wbh_ref[...], preferred_element_type=F32)
        ext = ext_ref[c.ext0:c.ext0 + POOL_HALO + t_rows, :]
        c.wsum = [jnp.dot(band_ref[g], ext[:, g * gd:(g + 1) * gd], preferred_element_type=F32)
                  for g in range(POOL_GROUPS)]

    def e_pooled(c):
        pos = j * step_rows + c.row0 + lax.broadcasted_iota(jnp.int32, (t_rows, gd), 0)
        c.pooled = []
        for g in range(POOL_GROUPS):
            cnt = jnp.minimum(pos + 1, POOL_WINDOWS[g]).astype(F32)
            c.pooled.append((c.wsum[g] / cnt - c.u[:, g * gd:(g + 1) * gd]).astype(BF16))

    def m_pool_mix(c):
        c.mixed = [jnp.dot(c.pooled[g], poolw_ref[g], preferred_element_type=F32) for g in range(POOL_GROUPS)]

    def e_gate(c):
        c.mixed = (jnp.concatenate(c.mixed, axis=1) * pools_ref[...]).astype(BF16)
        c.gate = _sigmoid(c.proj[:, c_gate:] + bgate_ref[...])

    def m_branch_pool(c):
        c.y_pool = jnp.dot(c.mixed, wbp_ref[...], preferred_element_type=F32)

    def e_merge(c):
        c.merged = (c.gate[:, 0:d] * c.y_hg + c.gate[:, d:2 * d] * c.y_pool).astype(BF16)

    def m_out(c):
        c.mix = jnp.dot(c.merged, wout_ref[...], preferred_element_type=F32)

    def e_residual(c):
        h1 = x_ref[c.rows, :] + gt1 * _rms_norm(c.mix, gpost_ref[...])
        h1_ref[c.rows, :] = h1
        hn2 = _rms_norm(h1, gffn_ref[...]) * (1.0 + sc2) + sh2
        packed = _pack_bf16_pairs(hn2)
        dq = hn2_ref.shape[2]
        hn2_ref[0, c.rows, :] = packed[:, 0:dq]
        hn2_ref[1, c.rows, :] = packed[:, dq:2 * dq]
        hn2_hi = hn2.astype(BF16)
        c.hn2_split = jnp.concatenate([hn2_hi, (hn2 - hn2_hi.astype(F32)).astype(BF16)], axis=0)

    def m_router(c):
        c.prod = jnp.dot(c.hn2_split, wr_ref[...], preferred_element_type=F32)

    def e_route(c):
        n_r = wr_ref.shape[1] // 2
        logits = (c.prod[0:t_rows, 0:n_r] + c.prod[t_rows:2 * t_rows, 0:n_r]
                  + c.prod[0:t_rows, n_r:2 * n_r])
        lt = logits.T + br_ref[...]
        lg = lt[0:MOE_GROUPS]
        gmax = jnp.max(lg, axis=0, keepdims=True)
        p_g = 1.0 / jnp.sum(jnp.exp(lg - gmax), axis=0, keepdims=True)
        gi = lax.broadcasted_iota(jnp.int32, lg.shape, 0).astype(F32)
        g_idx = jnp.min(jnp.where(lg == gmax, gi, float(MOE_GROUPS)), axis=0, keepdims=True)
        le = lt[8:8 + MOE_EPG]
        for g in range(1, MOE_GROUPS):
            le = jnp.where(g_idx == float(g), lt[8 + g * MOE_EPG:8 + (g + 1) * MOE_EPG], le)
        ei = lax.broadcasted_iota(jnp.int32, le.shape, 0).astype(F32)
        m1 = jnp.max(le, axis=0, keepdims=True)
        i1 = jnp.min(jnp.where(le == m1, ei, float(MOE_EPG)), axis=0, keepdims=True)
        le2 = jnp.where(ei == i1, -jnp.inf, le)
        m2 = jnp.max(le2, axis=0, keepdims=True)
        i2 = jnp.min(jnp.where(le2 == m2, ei, float(MOE_EPG)), axis=0, keepdims=True)
        r = jnp.exp(m2 - m1)
        w_first = p_g / (1.0 + r)
        e_first, e_second = g_idx * MOE_EPG + i1, g_idx * MOE_EPG + i2
        eid_ref[0:1, c.rows] = e_first.astype(jnp.int32)
        eid_ref[1:2, c.rows] = e_second.astype(jnp.int32)
        ex = lax.broadcasted_iota(jnp.int32, (N_EXPERTS, t_rows), 0).astype(F32)
        hits = jnp.where((ex == e_first) | (ex == e_second), 1.0, 0.0)
        cnt = jnp.broadcast_to(jnp.sum(hits, axis=1, keepdims=True), cnt_ref.shape)
        cnt_ref[...] = cnt if c.row0 == 0 else cnt_ref[...] + cnt
        rw_ref[0:1, c.rows] = w_first
        rw_ref[1:2, c.rows] = w_first * r

    stages = [m_project, e_gates, m_cumsum, e_decays, m_scores, e_states, m_outputs, e_head_norm,
              m_branch_hg, e_pooled, m_pool_mix, e_gate, m_branch_pool, e_merge, m_out, e_residual,
              m_router, e_route]

    subs = []
    for k in range(step_rows // t_rows):
        c = _Sub()
        c.row0 = k * t_rows
        c.rows = slice(c.row0, c.row0 + t_rows)
        c.ext0 = k * t_rows
        subs.append(c)
    for t in range(len(stages) + len(subs) - 1):
        for k, c in enumerate(subs):
            if 0 <= t - k < len(stages):
                stages[t - k](c)


def _band_matrices(t_rows):
    t = jnp.arange(t_rows)[:, None] + POOL_HALO
    jx = jnp.arange(t_rows + POOL_HALO)[None, :]
    return jnp.stack([((jx <= t) & (jx > t - w)) for w in POOL_WINDOWS]).astype(BF16)


def _chunk_tril(t_rows):
    r = jnp.arange(t_rows)[:, None]
    c = jnp.arange(t_rows)[None, :]
    return ((r >= c) & (r // CHUNK == c // CHUNK)).astype(BF16)


def _mixer(x, ada, g_pre, g_post, g_ffn, w1, b_gate, lb_logits, hg_norm_g, pool_w, pool_scale,
           w_bh, w_bp, w_out, w_router_t, b_router):
    bsz, seq, d = x.shape
    t_rows = MIX_ROWS
    step_rows = MIX_ROWS * MIX_SUBS
    n_t = seq // step_rows
    n_tok = bsz * seq
    hgw = hg_norm_g.shape[1]
    pw = pool_scale.shape[1]
    n_r = b_router.shape[0]

    def const(shape):
        return pl.BlockSpec(shape, lambda b, j: (0,) * len(shape), pipeline_mode=pl.Buffered(1))

    in_specs = [
        pl.BlockSpec((None, step_rows, d), lambda b, j: (b, j, 0)),
        pl.BlockSpec((None, 6, d), lambda b, j: (b, 0, 0)),
        const((1, d)), const((1, d)), const((1, d)),
        const(w1.shape), const((1, 2 * d)),
        const(lb_logits.shape), const((1, hgw)),
        const((t_rows, t_rows)), const((POOL_GROUPS, t_rows, t_rows + POOL_HALO)),
        const(pool_w.shape), const((1, pw)),
        const(w_bh.shape), const(w_bp.shape), const(w_out.shape),
        const(w_router_t.shape), const((n_r, 1)),
    ]
    out_specs = [
        pl.BlockSpec((None, step_rows, d), lambda b, j: (b, j, 0)),
        pl.BlockSpec((2, None, step_rows, d // 4), lambda b, j: (0, b, j, 0)),
        pl.BlockSpec((2, step_rows), lambda b, j: (0, b * n_t + j)),
        pl.BlockSpec((2, step_rows), lambda b, j: (0, b * n_t + j)),
        pl.BlockSpec((None, N_EXPERTS, 128), lambda b, j: (b * n_t + j, 0, 0)),
    ]
    out_shape = [
        jax.ShapeDtypeStruct((bsz, seq, d), F32),
        jax.ShapeDtypeStruct((2, bsz, seq, d // 4), jnp.uint32),
        jax.ShapeDtypeStruct((2, n_tok), jnp.int32),
        jax.ShapeDtypeStruct((2, n_tok), F32),
        jax.ShapeDtypeStruct((bsz * n_t, N_EXPERTS, 128), F32),
    ]
    scratch = [
        pltpu.VMEM((HG_HEADS, hgw // HG_HEADS, hgw // HG_HEADS), F32),
        pltpu.VMEM((step_rows + POOL_HALO, pw), BF16),
        pltpu.VMEM((step_rows, hgw), BF16),
    ]
    return pl.pallas_call(
        _mix_kernel,
        grid=(bsz, n_t),
        in_specs=in_specs, out_specs=out_specs, out_shape=out_shape, scratch_shapes=scratch,
        compiler_params=pltpu.CompilerParams(dimension_semantics=("arbitrary", "arbitrary"),
                                             vmem_limit_bytes=VMEM_LIMIT),
        name="mixer",
    )(x, ada, g_pre, g_post, g_ffn, w1, b_gate, lb_logits, hg_norm_g, _chunk_tril(t_rows),
      _band_matrices(t_rows), pool_w, pool_scale, w_bh, w_bp, w_out, w_router_t, b_router)


def _route_kernel(eid_ref, cnt_ref, dest_ref, blk_ref, meta_ref, run_ref):
    i = pl.program_id(0)
    cols = eid_ref.shape[1]
    eidx = lax.broadcasted_iota(jnp.int32, (N_EXPERTS, cols), 0)
    hot0 = eidx == eid_ref[0:1, :]
    hot1 = eidx == eid_ref[1:2, :]
    both = jnp.where(hot0 | hot1, 1.0, 0.0)
    tile_cnt = jnp.sum(both, axis=1, keepdims=True)

    @pl.when(i == 0)
    def _():
        total = jnp.sum(cnt_ref[...], axis=0)
        nblk_f = jnp.floor((total + float(MOE_BLOCK - 1)) * (1.0 / MOE_BLOCK))
        er = lax.broadcasted_iota(jnp.int32, (N_EXPERTS, N_EXPERTS), 0)
        ec = lax.broadcasted_iota(jnp.int32, (N_EXPERTS, N_EXPERTS), 1)
        lower = jnp.where(ec < er, 1.0, 0.0).astype(BF16)
        start_blk = jnp.dot(lower, nblk_f.astype(BF16), preferred_element_type=F32)
        run_ref[...] = start_blk[:, 0:1] * float(MOE_BLOCK)
        end_blk = start_blk + nblk_f
        lane = lax.broadcasted_iota(jnp.int32, (N_EXPERTS, blk_ref.shape[1]), 1).astype(F32)
        done = jnp.where(end_blk[:, 0:1] <= lane, 1.0, 0.0)
        blk_ref[...] = jnp.minimum(jnp.sum(done, axis=0, keepdims=True),
                                   float(N_EXPERTS - 1)).astype(jnp.int32)
        meta_ref[...] = jnp.broadcast_to(end_blk[N_EXPERTS - 1:N_EXPERTS, 0:1],
                                         meta_ref.shape).astype(jnp.int32)

    r = lax.broadcasted_iota(jnp.int32, (cols, cols), 0)
    c = lax.broadcasted_iota(jnp.int32, (cols, cols), 1)
    before = jnp.where(r < c, 1.0, 0.0).astype(BF16)
    prefix = jnp.dot(both.astype(BF16), before, preferred_element_type=F32)
    slot = run_ref[...] + prefix
    dest_ref[0:1, :] = jnp.sum(jnp.where(hot0, slot, 0.0), axis=0, keepdims=True).astype(jnp.int32)
    dest_ref[1:2, :] = jnp.sum(jnp.where(hot1, slot, 0.0), axis=0, keepdims=True).astype(jnp.int32)
    run_ref[...] += tile_cnt


def _route(eid, step_counts, n_blocks):
    n_tok = eid.shape[1]
    cols = ROUTE_COLS
    blk_lanes = pl.cdiv(n_blocks, 128) * 128
    return pl.pallas_call(
        _route_kernel,
        grid=(n_tok // cols,),
        in_specs=[pl.BlockSpec((2, cols), lambda i: (0, i)),
                  pl.BlockSpec(step_counts.shape, lambda i: (0, 0, 0))],
        out_specs=[pl.BlockSpec((2, cols), lambda i: (0, i)),
                   pl.BlockSpec((1, blk_lanes), lambda i: (0, 0)),
                   pl.BlockSpec((1, 128), lambda i: (0, 0))],
        out_shape=[jax.ShapeDtypeStruct((2, n_tok), jnp.int32),
                   jax.ShapeDtypeStruct((1, blk_lanes), jnp.int32),
                   jax.ShapeDtypeStruct((1, 128), jnp.int32)],
        scratch_shapes=[pltpu.VMEM((N_EXPERTS, 1), F32)],
        compiler_params=pltpu.CompilerParams(dimension_semantics=("arbitrary",)),
        name="route",
    )(eid, step_counts)


def _expert_kernel(blk_ref, meta_ref, xs_ref, wg_ref, wu_ref, wd_ref, ys_ref, wg_s, wu_s, wd_s):
    j = pl.program_id(0)
    used = j < meta_ref[0]
    first_of_expert = (j == 0) | (blk_ref[j] != blk_ref[jnp.maximum(j - 1, 0)])

    @pl.when(used & first_of_expert)
    def _():
        wg_s[...] = wg_ref[...].astype(BF16)
        wu_s[...] = wu_ref[...].astype(BF16)
        wd_s[...] = wd_ref[...].astype(BF16)

    @pl.when(used)
    def _():
        dq = ys_ref.shape[2]
        half = xs_ref.shape[1] // 2
        halves = [slice(0, half), slice(half, 2 * half)]

        def up_proj(rows):
            xb = _unpack_bf16_pairs(jnp.concatenate([xs_ref[0, rows, :], xs_ref[1, rows, :]], axis=1)).astype(BF16)
            return (jnp.dot(xb, wg_s[...], preferred_element_type=F32),
                    jnp.dot(xb, wu_s[...], preferred_element_type=F32))

        def activate(gp, up):
            return (gp * _sigmoid(gp) * up).astype(BF16)

        def down_proj(rows, act):
            packed = _pack_bf16_pairs(jnp.dot(act, wd_s[...], preferred_element_type=F32))
            ys_ref[0, rows, :] = packed[:, 0:dq]
            ys_ref[1, rows, :] = packed[:, dq:2 * dq]

        gu0 = up_proj(halves[0])
        act0 = activate(*gu0)
        gu1 = up_proj(halves[1])
        down_proj(halves[0], act0)
        act1 = activate(*gu1)
        down_proj(halves[1], act1)

    @pl.when(jnp.logical_not(used))
    def _():
        ys_ref[...] = jnp.zeros_like(ys_ref)


def _experts(block_e, n_used, xs, w_gate, w_up, w_down, n_blocks):
    dp = 2 * xs.shape[2]
    ff, d = w_down.shape[1], w_down.shape[2]

    def row_block(j, be, nu):
        return (0, jnp.minimum(j, nu[0] - 1), 0)

    def expert_block(j, be, nu):
        return (be[jnp.minimum(j, nu[0] - 1)], 0, 0)

    grid_spec = pltpu.PrefetchScalarGridSpec(
        num_scalar_prefetch=2,
        grid=(n_blocks,),
        in_specs=[pl.BlockSpec((2, MOE_BLOCK, dp // 2), row_block),
                  pl.BlockSpec((None, d, ff), expert_block),
                  pl.BlockSpec((None, d, ff), expert_block),
                  pl.BlockSpec((None, ff, d), expert_block)],
        out_specs=pl.BlockSpec((2, MOE_BLOCK, dp // 2), lambda j, be, nu: (0, j, 0)),
        scratch_shapes=[pltpu.VMEM((d, ff), BF16), pltpu.VMEM((d, ff), BF16), pltpu.VMEM((ff, d), BF16)],
    )
    return pl.pallas_call(
        _expert_kernel,
        grid_spec=grid_spec,
        out_shape=jax.ShapeDtypeStruct(xs.shape, xs.dtype),
        compiler_params=pltpu.CompilerParams(dimension_semantics=("arbitrary",),
                                             vmem_limit_bytes=VMEM_LIMIT),
        name="experts",
    )(block_e, n_used, xs, w_gate, w_up, w_down)


def _gather_rows_sc(table, idx):
    n, w = idx.shape[0], table.shape[1]
    mesh = plsc.VectorSubcoreMesh(core_axis_name="c", subcore_axis_name="s")

    @pl.kernel(out_type=jax.ShapeDtypeStruct((n, w), table.dtype), mesh=mesh, scratch_types=[])
    def gather(x_hbm, i_hbm, o_hbm):
        def body(i_vmem, o_vmem):
            pltpu.sync_copy(x_hbm.at[i_vmem.at[0]], o_vmem)

        pltpu.emit_pipeline(
            body,
            grid=(n // SC_WINDOW,),
            in_specs=[pl.BlockSpec((1, SC_WINDOW), lambda i: (0, i))],
            out_specs=[pl.BlockSpec((SC_WINDOW, w), lambda i: (i, 0))],
            core_axis_name=("c", "s"),
            dimension_semantics=(pltpu.PARALLEL,),
        )(i_hbm, o_hbm)

    return gather(table, idx.reshape(1, n))


def _scatter_rows_sc(src, idx, n_out):
    n, w = idx.shape[0], src.shape[1]
    n_src_windows = src.shape[0] // SC_WINDOW
    mesh = plsc.VectorSubcoreMesh(core_axis_name="c", subcore_axis_name="s")

    @pl.kernel(out_type=jax.ShapeDtypeStruct((n_out, w), src.dtype), mesh=mesh, scratch_types=[])
    def scatter(x_hbm, i_hbm, o_hbm):
        def body(x_vmem, i_vmem):
            pltpu.sync_copy(x_vmem, o_hbm.at[i_vmem.at[0]])

        pltpu.emit_pipeline(
            body,
            grid=(n // SC_WINDOW,),
            in_specs=[pl.BlockSpec((SC_WINDOW, w), lambda i: (i % n_src_windows, 0)),
                      pl.BlockSpec((1, SC_WINDOW), lambda i: (0, i))],
            out_specs=[],
            core_axis_name=("c", "s"),
            dimension_semantics=(pltpu.PARALLEL,),
        )(x_hbm, i_hbm)

    return scatter(src, idx.reshape(1, n))


def _combine_kernel(rows_ref, rw_ref, h1_ref, ada_ref, g_ref, out_ref):
    rw = rw_ref[...]

    def choice(k):
        return _unpack_bf16_pairs(jnp.concatenate([rows_ref[0, k], rows_ref[1, k]], axis=1))

    y = rw[:, 0:1] * choice(0) + rw[:, 1:2] * choice(1)
    gt2 = ada_ref[5:6, :]
    out_ref[...] = h1_ref[...] + gt2 * _rms_norm(y, g_ref[...])


def _combine(dest, ys, rw_cols, h1, ada, g_post, seq):
    n_tok, d = h1.shape
    rows = MOVE_ROWS
    per_batch = seq // rows
    n_half, n_rows, dq = ys.shape
    flat = dest.reshape(-1)
    idx = jnp.concatenate([flat + h * n_rows for h in range(n_half)])
    picked = _gather_rows_sc(ys.reshape(n_half * n_rows, dq), idx).reshape(n_half, 2, n_tok, dq)
    return pl.pallas_call(
        _combine_kernel,
        grid=(n_tok // rows,),
        in_specs=[pl.BlockSpec((n_half, 2, rows, dq), lambda i: (0, 0, i, 0)),
                  pl.BlockSpec((rows, 2), lambda i: (i, 0)),
                  pl.BlockSpec((rows, d), lambda i: (i, 0)),
                  pl.BlockSpec((None, 6, d), lambda i: (i // per_batch, 0, 0)),
                  pl.BlockSpec((1, d), lambda i: (0, 0))],
        out_specs=pl.BlockSpec((rows, d), lambda i: (i, 0)),
        out_shape=jax.ShapeDtypeStruct((n_tok, d), F32),
        compiler_params=pltpu.CompilerParams(dimension_semantics=("arbitrary",),
                                             vmem_limit_bytes=VMEM_LIMIT),
        name="combine",
    )(picked, rw_cols, h1, ada, g_post)


def kernel(x, c, w_ada, b_ada, g_pre_mix, g_post_mix, w_in, hg_lb_logits, hg_norm_g, pool_w, pool_scale,
           w_branch_hg, w_branch_pool, w_gate, b_gate, w_out, g_pre_ffn, g_post_ffn, w_router_group,
           b_router_group, w_router_expert, b_router_expert, w_exp_gate, w_exp_up, w_exp_down):
    depth = w_in.shape[0]
    bsz, seq, d = x.shape
    n_tok = bsz * seq
    n_blocks = -(-(n_tok * 2) // MOE_BLOCK) + N_EXPERTS
    assert seq % (MIX_ROWS * MIX_SUBS) == 0 and MIX_ROWS % CHUNK == 0 and seq % MOVE_ROWS == 0
    assert n_tok % ROUTE_COLS == 0 and hg_lb_logits.shape[0] == 2 and depth == 1

    h = x
    for l in range(depth):
        ada = _ada(c, w_ada[l], b_ada[l]).reshape(bsz, 6, d)
        w1 = jnp.concatenate([w_in[l], w_gate[l]], axis=1).astype(BF16)
        pad_g = jnp.zeros((d, 8 - MOE_GROUPS), F32)
        pad_e = jnp.zeros((d, ROUTER_COLS - 8 - N_EXPERTS), F32)
        w_router = jnp.concatenate([w_router_group[l], pad_g, w_router_expert[l], pad_e], axis=1)
        w_router_hi = w_router.astype(BF16)
        w_router_lo = (w_router - w_router_hi.astype(F32)).astype(BF16)
        w_router_t = jnp.concatenate([w_router_hi, w_router_lo], axis=1)
        b_router = jnp.concatenate(
            [b_router_group[l], jnp.zeros((8 - MOE_GROUPS,), F32), b_router_expert[l],
             jnp.zeros((ROUTER_COLS - 8 - N_EXPERTS,), F32)])[:, None]
        h1, hn2, eid, rw, step_counts = _mixer(
            h, ada, g_pre_mix[l][None], g_post_mix[l][None], g_pre_ffn[l][None], w1, b_gate[l][None],
            hg_lb_logits, hg_norm_g[l][None], pool_w[l].astype(BF16), pool_scale[l][None],
            w_branch_hg[l].astype(BF16), w_branch_pool[l].astype(BF16), w_out[l].astype(BF16),
            w_router_t, b_router)

        dest, block_e, meta = _route(eid, step_counts, n_blocks)
        p_rows = n_blocks * MOE_BLOCK
        idx = jnp.concatenate([dest[k] + h * p_rows for k in range(2) for h in range(2)])
        xs = _scatter_rows_sc(hn2.reshape(2 * n_tok, d // 4), idx, 2 * p_rows).reshape(2, p_rows, d // 4)
        ys = _experts(block_e[0, :n_blocks], meta[0, :1], xs, w_exp_gate[l], w_exp_up[l], w_exp_down[l],
                      n_blocks)
        h = _combine(dest, ys, rw.T, h1.reshape(n_tok, d), ada, g_post_ffn[l][None], seq)
        h = h.reshape(bsz, seq, d)
    return h
```
